```python
import jax
import jax.numpy as jnp
from jax import lax
import numpy as np

D_MODEL = 2048
BATCH = 16
SEQ = 2048
DEPTH = 4

HEAD_DIM = 128
SB_HEADS = 8
DSA_HEADS = 8
IDX_HEADS = 16
IDX_DIM = 64
TOPK_MAX = 256
D_FF = 5632
CONV_WIDTH = 3
Q_BLOCK = 128
ROPE_THETA = 10000.0
NORM_EPS = 1e-6
N_MOD = 6
SB_WIDTH = SB_HEADS * HEAD_DIM
DSA_WIDTH = DSA_HEADS * HEAD_DIM
IDX_Q_WIDTH = IDX_HEADS * IDX_DIM
IN_SPLIT_SIZES = (SB_WIDTH, SB_WIDTH, SB_WIDTH, DSA_WIDTH, DSA_WIDTH, DSA_WIDTH,
                  IDX_Q_WIDTH, IDX_DIM, IDX_HEADS, D_MODEL, D_MODEL)
IN_WIDTH = 3 * SB_WIDTH + 3 * DSA_WIDTH + IDX_Q_WIDTH + IDX_DIM + IDX_HEADS + 2 * D_MODEL

kernel_name = "hybrid_stickbreak_dsa_convffn_adaln"


def rmsnorm(x, g):
    xf = x.astype(jnp.float32)
    r = xf * lax.rsqrt(jnp.mean(xf * xf, axis=-1, keepdims=True) + NORM_EPS)
    return (r * g.astype(jnp.float32)).astype(x.dtype)


def rope(t, pos):
    d = t.shape[-1]
    inv_freq = 1.0 / (ROPE_THETA ** (jnp.arange(0, d, 2, dtype=jnp.float32) / d))
    ang = pos.astype(jnp.float32)[..., None] * inv_freq
    cos = jnp.cos(ang)[:, :, None, :]
    sin = jnp.sin(ang)[:, :, None, :]
    tf = t.astype(jnp.float32)
    t1, t2 = tf[..., : d // 2], tf[..., d // 2:]
    return jnp.concatenate([t1 * cos - t2 * sin, t2 * cos + t1 * sin], axis=-1).astype(t.dtype)


def stick_breaking_attention(q, k, v):
    B, S, H, dh = q.shape
    nb = S // Q_BLOCK
    scale = dh ** -0.5
    kpos = jnp.arange(S)
    qb = q.reshape(B, nb, Q_BLOCK, H, dh).transpose(1, 0, 2, 3, 4)

    def block(args):
        q_blk, start = args
        z = jnp.einsum('bqhd,bkhd->bhqk', q_blk, k,
                       preferred_element_type=jnp.float32) * scale
        qpos = start + jnp.arange(Q_BLOCK)
        past = (kpos[None, :] < qpos[:, None])[None, None]
        log1m = jnp.where(past, -jax.nn.softplus(z), 0.0)
        after = lax.cumsum(log1m, axis=3, reverse=True) - log1m
        w = jnp.where(past, jnp.exp(jax.nn.log_sigmoid(z) + after), 0.0)
        return jnp.einsum('bhqk,bkhd->bqhd', w.astype(v.dtype), v)

    starts = jnp.arange(nb, dtype=jnp.int32) * Q_BLOCK
    out = lax.map(block, (qb, starts))
    return out.transpose(1, 0, 2, 3, 4).reshape(B, S, H, dh)


def dsa_attention(q, k, v, qi, ki, wi):
    B, S, H, dh = q.shape
    nb = S // Q_BLOCK
    topk = min(TOPK_MAX, S // 4)
    kpos = jnp.arange(S)
    starts = jnp.arange(nb, dtype=jnp.int32) * Q_BLOCK

    def per_seq(args):
        q_s, k_s, v_s, qi_s, ki_s, wi_s = args
        q_blocks = q_s.reshape(nb, Q_BLOCK, H, dh)
        qi_blocks = qi_s.reshape(nb, Q_BLOCK, IDX_HEADS, IDX_DIM)
        wi_blocks = wi_s.reshape(nb, Q_BLOCK, IDX_HEADS)

        def block(bargs):
            q_b, qi_b, wi_b, start = bargs
            qpos = start + jnp.arange(Q_BLOCK)
            dots = jnp.einsum('qhi,ki->qhk', qi_b, ki_s,
                              preferred_element_type=jnp.float32) * (IDX_DIM ** -0.5)
            score = jnp.einsum('qh,qhk->qk', wi_b.astype(jnp.float32),
                               jax.nn.relu(dots)) * (IDX_HEADS ** -0.5)
            visible = kpos[None, :] <= qpos[:, None]
            score = jnp.where(visible, score, -jnp.inf)
            _, idx = lax.top_k(score, topk)
            k_sel = k_s[idx]
            v_sel = v_s[idx]
            logits = jnp.einsum('qhd,qkhd->qhk', q_b, k_sel,
                                preferred_element_type=jnp.float32) * (dh ** -0.5)
            valid = (idx <= qpos[:, None])[:, None, :]
            p = jax.nn.softmax(jnp.where(valid, logits, -jnp.inf), axis=-1)
            return jnp.einsum('qhk,qkhd->qhd', p.astype(v_s.dtype), v_sel)

        out = lax.map(block, (q_blocks, qi_blocks, wi_blocks, starts))
        return out.reshape(S, H, dh)

    return lax.map(per_seq, (q, k, v, qi, ki, wi))


def mixer(h, pos, w_in, w_a, w_b, w_o):
    B, S, _ = h.shape
    proj = h @ w_in
    offsets = [int(o) for o in np.cumsum(IN_SPLIT_SIZES)[:-1]]
    q_a, k_a, v_a, q_b, k_b, v_b, qi, ki, wi, g_a, g_b = jnp.split(proj, offsets, axis=-1)
    y_a = stick_breaking_attention(q_a.reshape(B, S, SB_HEADS, HEAD_DIM),
                                   k_a.reshape(B, S, SB_HEADS, HEAD_DIM),
                                   v_a.reshape(B, S, SB_HEADS, HEAD_DIM))
    y_a = y_a.reshape(B, S, SB_WIDTH) @ w_a
    y_b = dsa_attention(rope(q_b.reshape(B, S, DSA_HEADS, HEAD_DIM), pos),
                        rope(k_b.reshape(B, S, DSA_HEADS, HEAD_DIM), pos),
                        v_b.reshape(B, S, DSA_HEADS, HEAD_DIM),
                        rope(qi.reshape(B, S, IDX_HEADS, IDX_DIM), pos),
                        rope(ki[:, :, None, :], pos)[:, :, 0, :],
                        wi)
    y_b = y_b.reshape(B, S, DSA_WIDTH) @ w_b
    merged = jax.nn.sigmoid(g_a) * y_a + jax.nn.sigmoid(g_b) * y_b
    return merged @ w_o


def conv_ffn(h, w_up, conv_w, conv_b, w_down):
    u = h @ w_up
    S = u.shape[1]
    up = jnp.pad(u, ((0, 0), (CONV_WIDTH - 1, 0), (0, 0)))
    u = conv_b + sum(conv_w[i] * up[:, i:i + S] for i in range(CONV_WIDTH))
    a, b = jnp.split(u, 2, axis=-1)
    return (jax.nn.silu(a) * b) @ w_down


def setup_inputs(seed: int = 0) -> dict:
    key = jax.random.key(seed)
    ks = jax.random.split(key, 16)
    f32 = jnp.float32

    def nrm(k, shape, scale):
        return jax.random.normal(k, shape, f32) * scale

    x = nrm(ks[0], (BATCH, SEQ, D_MODEL), 1.0)
    c = nrm(ks[1], (BATCH, D_MODEL), 1.0)
    offset = jax.random.randint(ks[2], (BATCH, 1), 0, 4096, dtype=jnp.int32)
    positions = offset + jnp.arange(SEQ, dtype=jnp.int32)[None, :]
    w_in = nrm(ks[3], (DEPTH, D_MODEL, IN_WIDTH), D_MODEL ** -0.5)
    w_a = nrm(ks[4], (DEPTH, SB_WIDTH, D_MODEL), SB_WIDTH ** -0.5)
    w_b = nrm(ks[5], (DEPTH, DSA_WIDTH, D_MODEL), DSA_WIDTH ** -0.5)
    w_o = nrm(ks[6], (DEPTH, D_MODEL, D_MODEL), D_MODEL ** -0.5)
    w_ada = nrm(ks[7], (DEPTH, D_MODEL, N_MOD * D_MODEL), 0.1 * D_MODEL ** -0.5)
    b_ada = nrm(ks[8], (DEPTH, N_MOD * D_MODEL), 0.01)
    g_mix = 1.0 + nrm(ks[9], (DEPTH, D_MODEL), 0.02)
    g_ffn = 1.0 + nrm(ks[10], (DEPTH, D_MODEL), 0.02)
    w_up = nrm(ks[11], (DEPTH, D_MODEL, 2 * D_FF), D_MODEL ** -0.5)
    conv_w = nrm(ks[12], (DEPTH, CONV_WIDTH, 2 * D_FF), CONV_WIDTH ** -0.5)
    conv_b = nrm(ks[13], (DEPTH, 2 * D_FF), 0.01)
    w_down = nrm(ks[14], (DEPTH, D_FF, D_MODEL), D_FF ** -0.5)
    g_final = 1.0 + nrm(ks[15], (D_MODEL,), 0.02)
    return {"x": x, "c": c, "positions": positions, "w_in": w_in, "w_a": w_a, "w_b": w_b,
            "w_o": w_o, "w_ada": w_ada, "b_ada": b_ada, "g_mix": g_mix, "g_ffn": g_ffn,
            "w_up": w_up, "conv_w": conv_w, "conv_b": conv_b, "w_down": w_down,
            "g_final": g_final}


def reference(x, c, positions, w_in, w_a, w_b, w_o, w_ada, b_ada, g_mix, g_ffn,
              w_up, conv_w, conv_b, w_down, g_final):
    c_act = jax.nn.silu(c)
    for layer in range(DEPTH):
        mod = c_act @ w_ada[layer] + b_ada[layer]
        sh1, sc1, gt1, sh2, sc2, gt2 = jnp.split(mod[:, None, :], N_MOD, axis=-1)
        h = rmsnorm(x, g_mix[layer]) * (1 + sc1) + sh1
        x = x + (1 + gt1) * mixer(h, positions, w_in[layer], w_a[layer], w_b[layer], w_o[layer])
        h = rmsnorm(x, g_ffn[layer]) * (1 + sc2) + sh2
        x = x + (1 + gt2) * conv_ffn(h, w_up[layer], conv_w[layer], conv_b[layer], w_down[layer])
    return rmsnorm(x, g_final)
```

```python
import functools
import math

import jax
import jax.numpy as jnp
from jax import lax
from jax.experimental import pallas as pl
from jax.experimental.pallas import tpu as pltpu

HEAD_DIM = 128
IDX_DIM = 64
TOPK_MAX = 256
CONV_WIDTH = 3
ROPE_THETA = 10000.0
NORM_EPS = 1e-6
N_MOD = 6
LANES = 128
CHUNKS_PER_BLOCK = 4
PROJ_TN = LANES * CHUNKS_PER_BLOCK
VMEM_LIMIT = 56 * 1024 * 1024
INT_MIN = -2 ** 31
NEG_BIG = -1e30

_F32 = jnp.float32
_BF16 = jnp.bfloat16
_NT = (((1,), (1,)), ((), ()))


def _cparams(n_axes):
    return pltpu.CompilerParams(dimension_semantics=("arbitrary",) * n_axes,
                                vmem_limit_bytes=VMEM_LIMIT)


def _mod_kernel(c_ref, w_ref, b_ref, o_ref):
    c = c_ref[...]
    c_act = (c * jax.nn.sigmoid(c)).astype(_BF16)
    acc = jnp.dot(c_act, w_ref[...].astype(_BF16), preferred_element_type=_F32)
    o_ref[...] = acc + b_ref[...]


def _modulation(c, w_ada, b_ada):
    depth, d, n = w_ada.shape
    b = c.shape[0]
    tn = 1024 if n % 1024 == 0 else n
    return pl.pallas_call(
        _mod_kernel,
        out_shape=jax.ShapeDtypeStruct((depth, b, n), _F32),
        grid=(depth, n // tn),
        in_specs=[
            pl.BlockSpec((b, d), lambda l, j: (0, 0)),
            pl.BlockSpec((None, d, tn), lambda l, j: (l, 0, j)),
            pl.BlockSpec((None, 1, tn), lambda l, j: (l, 0, j)),
        ],
        out_specs=pl.BlockSpec((None, b, tn), lambda l, j: (l, 0, j)),
        compiler_params=_cparams(2),
        name="adaln_mod",
    )(c, w_ada, b_ada.reshape(depth, 1, n))


def _norm_modulate(x, g, sc, sh):
    ms = jnp.mean(x * x, axis=-1, keepdims=True)
    r = x * lax.rsqrt(ms + NORM_EPS)
    return (r * g) * (1.0 + sc) + sh


def _proj_layout(d, hs, hd, hi):
    sizes = [("ga", d // LANES), ("gb", d // LANES),
             ("qa", hs), ("ka", hs), ("va", hs),
             ("qb", hd), ("kb", hd), ("vb", hd),
             ("qi", hi * IDX_DIM // LANES), ("misc", CHUNKS_PER_BLOCK)]
    off, lay = 0, {}
    for name, n in sizes:
        assert n % CHUNKS_PER_BLOCK == 0, (name, n)
        lay[name] = (off, n)
        off += n
    lay["total"] = off
    return lay


def _rope128(t, cos, sin):
    return t * cos + pltpu.roll(t, HEAD_DIM // 2, axis=1) * sin


def _rope64(t, cos, sin_lo, sin_hi):
    half = IDX_DIM // 2
    return (t * cos + pltpu.roll(t, LANES - half, axis=1) * sin_lo
            + pltpu.roll(t, half, axis=1) * sin_hi)


def _in_proj_kernel(x_ref, g_ref, sc_ref, sh_ref, w_ref, c128_ref, s128_ref,
                    c64_ref, s64lo_ref, s64hi_ref, o_ref, wi_ref, h_ref, *, lay, idx_scale):
    j = pl.program_id(1)

    @pl.when(j == 0)
    def _():
        h = _norm_modulate(x_ref[...], g_ref[...], sc_ref[...], sh_ref[...])
        h_ref[...] = h.astype(_BF16)

    acc = jnp.dot(h_ref[...], w_ref[...], preferred_element_type=_F32)
    blk = lambda name: (lay[name][0] // CHUNKS_PER_BLOCK,
                        (lay[name][0] + lay[name][1]) // CHUNKS_PER_BLOCK)
    q_scale = HEAD_DIM ** -0.5

    def chunks(fn):
        for c in range(CHUNKS_PER_BLOCK):
            o_ref[c] = fn(acc[:, c * LANES:(c + 1) * LANES]).astype(_BF16)

    def in_group(*names):
        cond = None
        for nm in names:
            lo, hi = blk(nm)
            t = jnp.logical_and(j >= lo, j < hi)
            cond = t if cond is None else jnp.logical_or(cond, t)
        return cond

    @pl.when(in_group("ga", "gb"))
    def _():
        chunks(jax.nn.sigmoid)

    @pl.when(in_group("qa"))
    def _():
        chunks(lambda t: t * q_scale)

    @pl.when(in_group("ka", "va", "vb"))
    def _():
        chunks(lambda t: t)

    @pl.when(in_group("qb"))
    def _():
        chunks(lambda t: _rope128(t, c128_ref[...], s128_ref[...]) * q_scale)

    @pl.when(in_group("kb"))
    def _():
        chunks(lambda t: _rope128(t, c128_ref[...], s128_ref[...]))

    @pl.when(in_group("qi"))
    def _():
        chunks(lambda t: _rope64(t, c64_ref[...], s64lo_ref[...], s64hi_ref[...]))

    @pl.when(in_group("misc"))
    def _():
        for c in range(2):
            t = acc[:, c * LANES:(c + 1) * LANES]
            o_ref[c] = _rope64(t, c64_ref[...], s64lo_ref[...], s64hi_ref[...]).astype(_BF16)
        o_ref[2] = acc[:, 2 * LANES:3 * LANES].astype(_BF16)
        o_ref[3] = acc[:, 3 * LANES:4 * LANES].astype(_BF16)
        wi_ref[...] = acc[:, 2 * LANES:3 * LANES] * idx_scale


def _in_proj(x2, g, sc, sh, w, tabs, lay, hi, seq, tm):
    m, d = x2.shape
    nblk = lay["total"] // CHUNKS_PER_BLOCK
    tps = seq // tm
    row_spec = pl.BlockSpec((tm, LANES), lambda i, j: (i, 0))
    mod_spec = pl.BlockSpec((None, 1, d), lambda i, j: (i // tps, 0, 0))
    kern = functools.partial(_in_proj_kernel, lay=lay,
                             idx_scale=(IDX_DIM ** -0.5) * (hi ** -0.5))
    return pl.pallas_call(
        kern,
        out_shape=(jax.ShapeDtypeStruct((lay["total"], m, LANES), _BF16),
                   jax.ShapeDtypeStruct((m, LANES), _F32)),
        grid=(m // tm, nblk),
        in_specs=[
            pl.BlockSpec((tm, d), lambda i, j: (i, 0)),
            pl.BlockSpec((1, d), lambda i, j: (0, 0)),
            mod_spec, mod_spec,
            pl.BlockSpec((d, PROJ_TN), lambda i, j: (0, j)),
            row_spec, row_spec, row_spec, row_spec, row_spec,
        ],
        out_specs=(pl.BlockSpec((CHUNKS_PER_BLOCK, tm, LANES), lambda i, j: (j, i, 0)),
                   pl.BlockSpec((tm, LANES), lambda i, j: (i, 0))),
        scratch_shapes=[pltpu.VMEM((tm, d), _BF16)],
        compiler_params=_cparams(2),
        name="in_proj",
    )(x2, g, sc, sh, w, *tabs)


def _sb_kernel(q_ref, k_ref, v_ref, tri_ref, o_ref, acc_ref, carry_ref, *, t):
    qi = pl.program_id(2)
    q = q_ref[...]

    def tile(kb, diag):
        start = pl.multiple_of(kb * t, t)
        k = k_ref[pl.ds(start, t), :]
        v = v_ref[pl.ds(start, t), :]
        z = lax.dot_general(q, k, _NT, preferred_element_type=_F32)
        sp = jnp.maximum(z, 0.0) + jnp.log(1.0 + jnp.exp(-jnp.abs(z)))
        log1m = -sp
        if diag:
            past = (lax.broadcasted_iota(jnp.int32, (t, t), 1)
                    < lax.broadcasted_iota(jnp.int32, (t, t), 0))
            log1m = jnp.where(past, log1m, 0.0)
        cum = jnp.dot(log1m.astype(_BF16), tri_ref[...], preferred_element_type=_F32)
        after = cum[:, :t] + carry_ref[...]
        w = jnp.exp(z - sp + after)
        if diag:
            w = jnp.where(past, w, 0.0)
        acc_ref[...] += jnp.dot(w.astype(_BF16), v, preferred_element_type=_F32)
        carry_ref[...] += cum[:, t:]

    acc_ref[...] = jnp.zeros_like(acc_ref)
    carry_ref[...] = jnp.zeros_like(carry_ref)
    tile(qi, True)

    def body(it, c):
        tile(qi - 1 - it, False)
        return c

    lax.fori_loop(0, qi, body, 0)
    o_ref[...] = acc_ref[...].astype(_BF16)


def _sb_attention(proj, lay, batch, seq, hs, t):
    m = batch * seq
    nq = seq // t
    qa, ka, va = lay["qa"][0], lay["ka"][0], lay["va"][0]
    r = lax.broadcasted_iota(jnp.int32, (t, 2 * t), 0)
    c = lax.broadcasted_iota(jnp.int32, (t, 2 * t), 1)
    tri = jnp.where(jnp.logical_or(c >= t, r > c), 1.0, 0.0).astype(_BF16)
    return pl.pallas_call(
        functools.partial(_sb_kernel, t=t),
        out_shape=jax.ShapeDtypeStruct((m, hs * HEAD_DIM), _BF16),
        grid=(batch, hs, nq),
        in_specs=[
            pl.BlockSpec((None, t, HEAD_DIM), lambda b, h, i: (qa + h, b * nq + i, 0)),
            pl.BlockSpec((None, seq, HEAD_DIM), lambda b, h, i: (ka + h, b, 0)),
            pl.BlockSpec((None, seq, HEAD_DIM), lambda b, h, i: (va + h, b, 0)),
            pl.BlockSpec((t, 2 * t), lambda b, h, i: (0, 0)),
        ],
        out_specs=pl.BlockSpec((t, HEAD_DIM), lambda b, h, i: (b * nq + i, h)),
        scratch_shapes=[pltpu.VMEM((t, HEAD_DIM), _F32), pltpu.VMEM((t, t), _F32)],
        compiler_params=_cparams(3),
        name="sb_attn",
    )(proj, proj, proj, tri)


def _sortable(score):
    u = lax.bitcast_convert_type(score, jnp.int32)
    return u ^ (lax.shift_right_arithmetic(u, 31) & 0x7FFFFFFF)


def _dsa_kernel(qi_ref, ke_ref, wi_ref, qb_ref, kb_ref, vb_ref, o_ref,
                wb_ref, score_ref, key_ref, bias_ref, *, tq, seq, hi, hd, topk, nbuckets):
    i = pl.program_id(1)
    npairs = hi // 2

    w = wi_ref[...]
    for h in range(hi):
        wb_ref[h] = jnp.broadcast_to(w[:, h:h + 1], (tq, LANES))

    def run(width):
        reps = width // LANES
        row = i * tq + lax.broadcasted_iota(jnp.int32, (tq, width), 0)
        col = lax.broadcasted_iota(jnp.int32, (tq, width), 1)
        visible = col <= row

        score_ref[:, :width] = jnp.zeros((tq, width), _F32)

        def pair(p, c):
            qp = qi_ref[p]
            d_even = lax.dot_general(qp, ke_ref[0, :width, :], _NT, preferred_element_type=_F32)
            d_odd = lax.dot_general(qp, ke_ref[1, :width, :], _NT, preferred_element_type=_F32)
            w_even = jnp.tile(wb_ref[2 * p], (1, reps))
            w_odd = jnp.tile(wb_ref[2 * p + 1], (1, reps))
            score_ref[:, :width] += (w_even * jnp.maximum(d_even, 0.0)
                                     + w_odd * jnp.maximum(d_odd, 0.0))
            return c

        lax.fori_loop(0, npairs, pair, 0)
        key_ref[:, :width] = jnp.where(visible, _sortable(score_ref[:, :width]), INT_MIN)

        def count_ge(cand):
            hit = jnp.where(key_ref[:, :width] >= cand, 1.0, 0.0)
            return jnp.sum(hit, axis=1, keepdims=True)

        kf = float(topk)
        prefix = jnp.where(count_ge(jnp.zeros((tq, 1), jnp.int32)) >= kf, 0, INT_MIN)
        prefix = prefix.astype(jnp.int32)

        def step(b, prefix):
            cand = prefix + lax.shift_left(jnp.int32(1), 30 - b)
            return jnp.where(count_ge(cand) >= kf, cand, prefix)

        thr = lax.fori_loop(0, 31, step, prefix)
        thr = jnp.maximum(thr, INT_MIN + 1)
        bias_ref[:, :width] = jnp.where(key_ref[:, :width] >= thr, 0.0, NEG_BIG)

        def head(h, c):
            s = lax.dot_general(qb_ref[h], kb_ref[h, :width, :], _NT, preferred_element_type=_F32)
            s = s + bias_ref[:, :width]
            mx = jnp.max(s, axis=1, keepdims=True)
            p = jnp.exp(s - mx)
            den = jnp.sum(p, axis=1, keepdims=True)
            o = jnp.dot(p.astype(_BF16), vb_ref[h, :width, :], preferred_element_type=_F32)
            o_ref[h] = (o / den).astype(_BF16)
            return c

        lax.fori_loop(0, hd, head, 0)

    nq = seq // tq
    per = nq // nbuckets
    bucket = i // per
    for bi in range(nbuckets):
        pl.when(bucket == bi)(functools.partial(run, (bi + 1) * per * tq))


def _dsa_attention(proj, wi, lay, batch, seq, hi, hd, tq):
    m = batch * seq
    nq = seq // tq
    topk = min(TOPK_MAX, seq // 4)
    nbuckets = 4 if nq % 4 == 0 else 1
    npairs = hi // 2
    qi_b = lay["qi"][0] // npairs
    ke_b = lay["misc"][0] // 2
    qb_b, kb_b, vb_b = (lay[n][0] // hd for n in ("qb", "kb", "vb"))
    assert lay["qi"][0] % npairs == 0 and lay["misc"][0] % 2 == 0
    assert all(lay[n][0] % hd == 0 for n in ("qb", "kb", "vb"))
    kern = functools.partial(_dsa_kernel, tq=tq, seq=seq, hi=hi, hd=hd, topk=topk,
                             nbuckets=nbuckets)
    return pl.pallas_call(
        kern,
        out_shape=jax.ShapeDtypeStruct((hd, m, HEAD_DIM), _BF16),
        grid=(batch, nq),
        in_specs=[
            pl.BlockSpec((npairs, tq, LANES), lambda b, i: (qi_b, b * nq + i, 0)),
            pl.BlockSpec((2, seq, LANES), lambda b, i: (ke_b, b, 0)),
            pl.BlockSpec((tq, LANES), lambda b, i: (b * nq + i, 0)),
            pl.BlockSpec((hd, tq, HEAD_DIM), lambda b, i: (qb_b, b * nq + i, 0)),
            pl.BlockSpec((hd, seq, HEAD_DIM), lambda b, i: (kb_b, b, 0)),
            pl.BlockSpec((hd, seq, HEAD_DIM), lambda b, i: (vb_b, b, 0)),
        ],
        out_specs=pl.BlockSpec((hd, tq, HEAD_DIM), lambda b, i: (0, b * nq + i, 0)),
        scratch_shapes=[pltpu.VMEM((hi, tq, LANES), _F32),
                        pltpu.VMEM((tq, seq), _F32),
                        pltpu.VMEM((tq, seq), jnp.int32),
                        pltpu.VMEM((tq, seq), _F32)],
        compiler_params=_cparams(2),
        name="dsa_attn",
    )(proj, proj, wi, proj, proj, proj)


def _merge_kernel(ya_ref, yb_ref, wa_ref, wb_ref, ga_ref, gb_ref, o_ref, *, hd):
    yb = jnp.concatenate([yb_ref[h] for h in range(hd)], axis=1)
    a = jnp.dot(ya_ref[...], wa_ref[...], preferred_element_type=_F32)
    b = jnp.dot(yb, wb_ref[...], preferred_element_type=_F32)
    for c in range(CHUNKS_PER_BLOCK):
        sl = slice(c * LANES, (c + 1) * LANES)
        o_ref[:, sl] = (ga_ref[c].astype(_F32) * a[:, sl]
                        + gb_ref[c].astype(_F32) * b[:, sl]).astype(_BF16)


def _merge(ya, yb, wa, wb, proj, lay, tm):
    m, ka = ya.shape
    hd = yb.shape[0]
    d = wa.shape[1]
    ga_b = lay["ga"][0] // CHUNKS_PER_BLOCK
    gb_b = lay["gb"][0] // CHUNKS_PER_BLOCK
    gate = lambda off: pl.BlockSpec((CHUNKS_PER_BLOCK, tm, LANES), lambda i, j: (off + j, i, 0))
    return pl.pallas_call(
        functools.partial(_merge_kernel, hd=hd),
        out_shape=jax.ShapeDtypeStruct((m, d), _BF16),
        grid=(m // tm, d // PROJ_TN),
        in_specs=[
            pl.BlockSpec((tm, ka), lambda i, j: (i, 0)),
            pl.BlockSpec((hd, tm, HEAD_DIM), lambda i, j: (0, i, 0)),
            pl.BlockSpec((ka, PROJ_TN), lambda i, j: (0, j)),
            pl.BlockSpec((hd * HEAD_DIM, PROJ_TN), lambda i, j: (0, j)),
            gate(ga_b), gate(gb_b),
        ],
        out_specs=pl.BlockSpec((tm, PROJ_TN), lambda i, j: (i, j)),
        compiler_params=_cparams(2),
        name="merge",
    )(ya, yb, wa, wb, proj, proj)


def _out_res_kernel(a_ref, w_ref, x_ref, gt_ref, o_ref):
    acc = jnp.dot(a_ref[...], w_ref[...], preferred_element_type=_F32)
    o_ref[...] = x_ref[...] + (1.0 + gt_ref[...]) * acc


def _out_res(a, w, x2, gt, seq, tm, tn):
    m, k = a.shape
    d = w.shape[1]
    tps = seq // tm
    return pl.pallas_call(
        _out_res_kernel,
        out_shape=jax.ShapeDtypeStruct((m, d), _F32),
        grid=(m // tm, d // tn),
        in_specs=[
            pl.BlockSpec((tm, k), lambda i, j: (i, 0)),
            pl.BlockSpec((k, tn), lambda i, j: (0, j)),
            pl.BlockSpec((tm, tn), lambda i, j: (i, j)),
            pl.BlockSpec((None, 1, tn), lambda i, j: (i // tps, 0, j)),
        ],
        out_specs=pl.BlockSpec((tm, tn), lambda i, j: (i, j)),
        compiler_params=_cparams(2),
        name="out_res",
    )(a, w, x2, gt)


def _ffn_up_kernel(x_ref, g_ref, sc_ref, sh_ref, w_ref, cw_ref, cb_ref, o_ref,
                   h_ref, u_ref, carry_ref, *, tm, tn, tps):
    i = pl.program_id(0)
    j = pl.program_id(1)
    pad = 8

    @pl.when(j == 0)
    def _():
        h = _norm_modulate(x_ref[...], g_ref[...], sc_ref[...], sh_ref[...])
        h_ref[...] = h.astype(_BF16)

    u = jnp.dot(h_ref[...], w_ref[...], preferred_element_type=_F32)
    @pl.when(i % tps == 0)
    def _():
        u_ref[0:pad, :] = jnp.zeros((pad, 2 * tn), _F32)

    @pl.when(i % tps != 0)
    def _():
        u_ref[0:pad, :] = carry_ref[j]

    u_ref[pad:pad + tm, :] = u
    carry_ref[j] = u[tm - pad:tm, :]
    cw = cw_ref[...]
    u1 = u_ref[pad - 1:pad - 1 + tm, :]
    u2 = u_ref[pad - 2:pad - 2 + tm, :]
    v = cb_ref[...] + (cw[0:1, :] * u2 + cw[1:2, :] * u1 + cw[2:3, :] * u)
    a = v[:, :tn]
    b = v[:, tn:]
    o_ref[...] = (a * jax.nn.sigmoid(a) * b).astype(_BF16)


def _ffn_up(x2, g, sc, sh, w, cw, cb, seq, tm, tn):
    m, d = x2.shape
    nblk = w.shape[1] // (2 * tn)
    tps = seq // tm
    mod_spec = pl.BlockSpec((None, 1, d), lambda i, j: (i // tps, 0, 0))
    return pl.pallas_call(
        functools.partial(_ffn_up_kernel, tm=tm, tn=tn, tps=tps),
        out_shape=jax.ShapeDtypeStruct((m, nblk * tn), _BF16),
        grid=(m // tm, nblk),
        in_specs=[
            pl.BlockSpec((tm, d), lambda i, j: (i, 0)),
            pl.BlockSpec((1, d), lambda i, j: (0, 0)),
            mod_spec, mod_spec,
            pl.BlockSpec((d, 2 * tn), lambda i, j: (0, j)),
            pl.BlockSpec((CONV_WIDTH, 2 * tn), lambda i, j: (0, j)),
            pl.BlockSpec((1, 2 * tn), lambda i, j: (0, j)),
        ],
        out_specs=pl.BlockSpec((tm, tn), lambda i, j: (i, j)),
        scratch_shapes=[pltpu.VMEM((tm, d), _BF16),
                        pltpu.VMEM((tm + 8, 2 * tn), _F32),
                        pltpu.VMEM((nblk, 8, 2 * tn), _F32)],
        compiler_params=_cparams(2),
        name="ffn_up",
    )(x2, g, sc, sh, w, cw, cb)


def _final_norm_kernel(x_ref, g_ref, o_ref):
    x = x_ref[...]
    ms = jnp.mean(x * x, axis=-1, keepdims=True)
    o_ref[...] = (x * lax.rsqrt(ms + NORM_EPS)) * g_ref[...]


def _final_norm(x2, g, tm):
    m, d = x2.shape
    return pl.pallas_call(
        _final_norm_kernel,
        out_shape=jax.ShapeDtypeStruct((m, d), _F32),
        grid=(m // tm,),
        in_specs=[pl.BlockSpec((tm, d), lambda i: (i, 0)),
                  pl.BlockSpec((1, d), lambda i: (0, 0))],
        out_specs=pl.BlockSpec((tm, d), lambda i: (i, 0)),
        compiler_params=_cparams(1),
        name="final_norm",
    )(x2, g)


def _arrange_w_in(w_in, d, hs, hd, hi):
    sw, dw, iw = hs * HEAD_DIM, hd * HEAD_DIM, hi * IDX_DIM
    sizes = (sw, sw, sw, dw, dw, dw, iw, IDX_DIM, hi, d, d)
    offs = [0]
    for s in sizes:
        offs.append(offs[-1] + s)
    qa, ka, va, qb, kb, vb, qi, ki, wi, ga, gb = (
        w_in[..., offs[n]:offs[n + 1]] for n in range(len(sizes)))
    z = lambda n: jnp.zeros(w_in.shape[:-1] + (n,), w_in.dtype)
    misc = jnp.concatenate([ki, z(LANES - IDX_DIM), z(LANES - IDX_DIM), ki,
                            wi, z(LANES - hi), z(LANES)], axis=-1)
    return jnp.concatenate([ga, gb, qa, ka, va, qb, kb, vb, qi, misc], axis=-1).astype(_BF16)


def _rope_tables(positions):
    pos = positions.reshape(-1).astype(_F32)

    def cos_sin(dim):
        inv_freq = 1.0 / (ROPE_THETA ** (jnp.arange(0, dim, 2, dtype=_F32) / dim))
        ang = pos[:, None] * inv_freq
        return jnp.cos(ang), jnp.sin(ang)

    c, s = cos_sin(HEAD_DIM)
    c128 = jnp.concatenate([c, c], axis=-1)
    s128 = jnp.concatenate([-s, s], axis=-1)
    c, s = cos_sin(IDX_DIM)
    zero = jnp.zeros_like(s)
    c64 = jnp.concatenate([c, c, c, c], axis=-1)
    s64lo = jnp.concatenate([-s, zero, -s, zero], axis=-1)
    s64hi = jnp.concatenate([zero, s, zero, s], axis=-1)
    return c128, s128, c64, s64lo, s64hi


def _arrange_ffn(w_up, conv_w, conv_b, tn):
    dff = w_up.shape[-1] // 2
    nblk = dff // tn

    def inter(t):
        lead = t.shape[:-1]
        t = t.reshape(lead + (2, nblk, tn))
        t = jnp.swapaxes(t, -3, -2)
        return t.reshape(lead + (2 * dff,))

    return inter(w_up).astype(_BF16), inter(conv_w), inter(conv_b)


def _pick(n, prefs):
    for p in prefs:
        if n % p == 0:
            return p
    return n


def kernel(x, c, positions, w_in, w_a, w_b, w_o, w_ada, b_ada, g_mix, g_ffn, w_up, conv_w,
           conv_b, w_down, g_final):
    batch, seq, d = x.shape
    depth = w_in.shape[0]
    hs = w_a.shape[1] // HEAD_DIM
    hd = w_b.shape[1] // HEAD_DIM
    rest = w_in.shape[2] - 3 * hs * HEAD_DIM - 3 * hd * HEAD_DIM - 2 * d - IDX_DIM
    hi = rest // (IDX_DIM + 1)
    dff = w_down.shape[1]
    m = batch * seq
    lay = _proj_layout(d, hs, hd, hi)

    tm = _pick(seq, (1024, 512, 256, 128))
    t_sb = _pick(seq, (256, 128))
    tq_dsa = 128
    tn_ffn = _pick(dff, (512, 256, 128))
    tn_out = _pick(d, (512, 256, 128))

    w_in_r = _arrange_w_in(w_in, d, hs, hd, hi)
    w_up_r, conv_w_r, conv_b_r = _arrange_ffn(w_up, conv_w, conv_b, tn_ffn)
    w_a16, w_b16, w_o16, w_down16 = (t.astype(_BF16) for t in (w_a, w_b, w_o, w_down))
    tabs = _rope_tables(positions)
    mod = _modulation(c, w_ada, b_ada).reshape(depth, batch, N_MOD, 1, d)

    x2 = x.reshape(m, d)
    for l in range(depth):
        sh1, sc1, gt1, sh2, sc2, gt2 = (mod[l, :, n] for n in range(N_MOD))
        proj, wi = _in_proj(x2, g_mix[l].reshape(1, d), sc1, sh1, w_in_r[l], tabs, lay, hi,
                            seq, tm)
        ya = _sb_attention(proj, lay, batch, seq, hs, t_sb)
        yb = _dsa_attention(proj, wi, lay, batch, seq, hi, hd, tq_dsa)
        merged = _merge(ya, yb, w_a16[l], w_b16[l], proj, lay, tm)
        x2 = _out_res(merged, w_o16[l], x2, gt1, seq, tm, tn_out)
        g = _ffn_up(x2, g_ffn[l].reshape(1, d), sc2, sh2, w_up_r[l], conv_w_r[l],
                    conv_b_r[l].reshape(1, -1), seq, tm, tn_ffn)
        x2 = _out_res(g, w_down16[l], x2, gt2, seq, tm, tn_out)
    out = _final_norm(x2, g_final.reshape(1, d), tm)
    return out.reshape(batch, seq, d)
```

```python
import functools
import math

import jax
import jax.numpy as jnp
from jax import lax
from jax.experimental import pallas as pl
from jax.experimental.pallas import tpu as pltpu

HEAD_DIM = 128
IDX_DIM = 64
TOPK_MAX = 256
CONV_WIDTH = 3
ROPE_THETA = 10000.0
NORM_EPS = 1e-6
N_MOD = 6
LANES = 128
SUBLANES = 8
CHUNKS_PER_BLOCK = 4
PROJ_TN = LANES * CHUNKS_PER_BLOCK
VMEM_LIMIT = 56 * 1024 * 1024
INT_MIN = -2 ** 31
I16_MIN, I16_MAX = -2 ** 15, 2 ** 15 - 1
PACK16 = 16
NEG_BIG = -1e30
ROW_CHUNK = 256
LOG2E = math.log2(math.e)

_F32 = jnp.float32
_BF16 = jnp.bfloat16
_NT = (((1,), (1,)), ((), ()))


def _cparams(n_axes):
    return pltpu.CompilerParams(dimension_semantics=("arbitrary",) * n_axes,
                                vmem_limit_bytes=VMEM_LIMIT)


def _mod_kernel(c_ref, w_ref, b_ref, o_ref):
    c = c_ref[...]
    c_act = (c * jax.nn.sigmoid(c)).astype(_BF16)
    acc = jnp.dot(c_act, w_ref[...].astype(_BF16), preferred_element_type=_F32)
    o_ref[...] = acc + b_ref[...]


def _modulation(c, w_ada, b_ada):
    depth, d, n = w_ada.shape
    b = c.shape[0]
    tn = 1024 if n % 1024 == 0 else n
    return pl.pallas_call(
        _mod_kernel,
        out_shape=jax.ShapeDtypeStruct((depth, b, n), _F32),
        grid=(depth, n // tn),
        in_specs=[
            pl.BlockSpec((b, d), lambda l, j: (0, 0)),
            pl.BlockSpec((None, d, tn), lambda l, j: (l, 0, j)),
            pl.BlockSpec((None, 1, tn), lambda l, j: (l, 0, j)),
        ],
        out_specs=pl.BlockSpec((None, b, tn), lambda l, j: (l, 0, j)),
        compiler_params=_cparams(2),
        name="adaln_mod",
    )(c, w_ada, b_ada.reshape(depth, 1, n))


def _norm_modulate(x, g, sc, sh):
    ms = jnp.mean(x * x, axis=-1, keepdims=True)
    r = x * lax.rsqrt(ms + NORM_EPS)
    return (r * g) * (1.0 + sc) + sh


def _proj_layout(d, hs, hd, hi):
    sizes = [("ga", d // LANES), ("gb", d // LANES),
             ("qa", hs), ("ka", hs), ("va", hs),
             ("qb", hd), ("kb", hd), ("vb", hd),
             ("qi", hi * IDX_DIM // LANES), ("misc", CHUNKS_PER_BLOCK)]
    off, lay = 0, {}
    for name, n in sizes:
        assert n % CHUNKS_PER_BLOCK == 0, (name, n)
        lay[name] = (off, n)
        off += n
    lay["total"] = off
    return lay


def _rope128(t, cos, sin):
    return t * cos + pltpu.roll(t, HEAD_DIM // 2, axis=1) * sin


def _rope64(t, cos, sin_lo, sin_hi):
    half = IDX_DIM // 2
    return (t * cos + pltpu.roll(t, LANES - half, axis=1) * sin_lo
            + pltpu.roll(t, half, axis=1) * sin_hi)


def _in_proj_kernel(x_ref, g_ref, sc_ref, sh_ref, w_ref, c128_ref, s128_ref,
                    c64_ref, s64lo_ref, s64hi_ref, o_ref, wi_ref, h_ref, *, lay, idx_scale, tm):
    j = pl.program_id(1)

    @pl.when(j == 0)
    def _():
        h = _norm_modulate(x_ref[...], g_ref[...], sc_ref[...], sh_ref[...])
        h_ref[...] = h.astype(_BF16)

    blk = lambda name: (lay[name][0] // CHUNKS_PER_BLOCK,
                        (lay[name][0] + lay[name][1]) // CHUNKS_PER_BLOCK)
    qa_scale = (HEAD_DIM ** -0.5) * LOG2E
    qb_scale = HEAD_DIM ** -0.5
    rc = min(ROW_CHUNK, tm)

    def run(epilogue):
        for r in range(tm // rc):
            rows = slice(r * rc, (r + 1) * rc)
            acc = jnp.dot(h_ref[rows, :], w_ref[...], preferred_element_type=_F32)
            epilogue(acc, rows)

    def per_chunk(fn):
        def epilogue(acc, rows):
            for c in range(CHUNKS_PER_BLOCK):
                o_ref[c, rows, :] = fn(acc[:, c * LANES:(c + 1) * LANES], rows).astype(_BF16)
        return epilogue

    def in_group(*names):
        cond = None
        for nm in names:
            lo, hi = blk(nm)
            t = jnp.logical_and(j >= lo, j < hi)
            cond = t if cond is None else jnp.logical_or(cond, t)
        return cond

    rope128 = lambda t, rows: _rope128(t, c128_ref[rows, :], s128_ref[rows, :])
    rope64 = lambda t, rows: _rope64(t, c64_ref[rows, :], s64lo_ref[rows, :], s64hi_ref[rows, :])

    @pl.when(in_group("ga", "gb"))
    def _():
        run(per_chunk(lambda t, rows: jax.nn.sigmoid(t)))

    @pl.when(in_group("qa"))
    def _():
        run(per_chunk(lambda t, rows: t * qa_scale))

    @pl.when(in_group("ka", "va", "vb"))
    def _():
        run(per_chunk(lambda t, rows: t))

    @pl.when(in_group("qb"))
    def _():
        run(per_chunk(lambda t, rows: rope128(t, rows) * qb_scale))

    @pl.when(in_group("kb"))
    def _():
        run(per_chunk(rope128))

    @pl.when(in_group("qi"))
    def _():
        run(per_chunk(rope64))

    @pl.when(in_group("misc"))
    def _():
        def epilogue(acc, rows):
            for c in range(2):
                o_ref[c, rows, :] = rope64(acc[:, c * LANES:(c + 1) * LANES], rows).astype(_BF16)
            o_ref[2, rows, :] = acc[:, 2 * LANES:3 * LANES].astype(_BF16)
            o_ref[3, rows, :] = acc[:, 3 * LANES:4 * LANES].astype(_BF16)
            wi_ref[rows, :] = acc[:, 2 * LANES:3 * LANES] * idx_scale
        run(epilogue)


def _in_proj(x2, g, sc, sh, w, tabs, lay, hi, seq, tm):
    m, d = x2.shape
    nblk = lay["total"] // CHUNKS_PER_BLOCK
    tps = seq // tm
    row_spec = pl.BlockSpec((tm, LANES), lambda i, j: (i, 0))
    mod_spec = pl.BlockSpec((None, 1, d), lambda i, j: (i // tps, 0, 0))
    kern = functools.partial(_in_proj_kernel, lay=lay, tm=tm,
                             idx_scale=(IDX_DIM ** -0.5) * (hi ** -0.5))
    return pl.pallas_call(
        kern,
        out_shape=(jax.ShapeDtypeStruct((lay["total"], m, LANES), _BF16),
                   jax.ShapeDtypeStruct((m, LANES), _F32)),
        grid=(m // tm, nblk),
        in_specs=[
            pl.BlockSpec((tm, d), lambda i, j: (i, 0)),
            pl.BlockSpec((1, d), lambda i, j: (0, 0)),
            mod_spec, mod_spec,
            pl.BlockSpec((d, PROJ_TN), lambda i, j: (0, j)),
            row_spec, row_spec, row_spec, row_spec, row_spec,
        ],
        out_specs=(pl.BlockSpec((CHUNKS_PER_BLOCK, tm, LANES), lambda i, j: (j, i, 0)),
                   pl.BlockSpec((tm, LANES), lambda i, j: (i, 0))),
        scratch_shapes=[pltpu.VMEM((tm, d), _BF16)],
        compiler_params=_cparams(2),
        name="in_proj",
    )(x2, g, sc, sh, w, *tabs)


def _add_lane_replicated(x, rep):
    n = x.shape[1] // LANES
    return jnp.concatenate([x[:, c * LANES:(c + 1) * LANES] + rep for c in range(n)], axis=1)


def _sb_kernel(q_ref, k_ref, v_ref, tri_ref, o_ref, *, t, nh, nq):
    qi = pl.program_id(2)

    def variant(n):
        past = (lax.broadcasted_iota(jnp.int32, (t, t), 1)
                < lax.broadcasted_iota(jnp.int32, (t, t), 0))
        for hh in range(nh):
            q = q_ref[hh]
            acc = jnp.zeros((t, HEAD_DIM), _F32)
            carry = jnp.zeros((t, LANES), _F32)
            for kb in range(n, -1, -1):
                diag = kb == n
                k = k_ref[hh, kb * t:(kb + 1) * t, :]
                v = v_ref[hh, kb * t:(kb + 1) * t, :]
                z = lax.dot_general(q, k, _NT, preferred_element_type=_F32)
                sp = jnp.maximum(z, 0.0) + jnp.log(1.0 + jnp.exp2(-jnp.abs(z))) * LOG2E
                log1m = -sp
                if diag:
                    log1m = jnp.where(past, log1m, 0.0)
                l1m = log1m.astype(_BF16)
                cum = jnp.dot(l1m, tri_ref[...], preferred_element_type=_F32)
                w = jnp.exp2(z - sp + _add_lane_replicated(cum, carry))
                if diag:
                    w = jnp.where(past, w, 0.0)
                acc = acc + jnp.dot(w.astype(_BF16), v, preferred_element_type=_F32)
                if kb > 0:
                    total = cum[:, 0:1] + l1m[:, 0:1].astype(_F32)
                    carry = carry + jnp.broadcast_to(total, (t, LANES))
            o_ref[:, hh * HEAD_DIM:(hh + 1) * HEAD_DIM] = acc.astype(_BF16)

    for n in range(nq):
        pl.when(qi == n)(functools.partial(variant, n))


def _sb_attention(proj, lay, batch, seq, hs, t, nh):
    m = batch * seq
    nq = seq // t
    qa, ka, va = (lay[n][0] // nh for n in ("qa", "ka", "va"))
    assert hs % nh == 0 and all(lay[n][0] % nh == 0 for n in ("qa", "ka", "va"))
    r = lax.broadcasted_iota(jnp.int32, (t, t), 0)
    c = lax.broadcasted_iota(jnp.int32, (t, t), 1)
    tri = jnp.where(r > c, 1.0, 0.0).astype(_BF16)
    return pl.pallas_call(
        functools.partial(_sb_kernel, t=t, nh=nh, nq=nq),
        out_shape=jax.ShapeDtypeStruct((m, hs * HEAD_DIM), _BF16),
        grid=(batch, hs // nh, nq),
        in_specs=[
            pl.BlockSpec((nh, t, HEAD_DIM), lambda b, h, i: (qa + h, b * nq + i, 0)),
            pl.BlockSpec((nh, seq, HEAD_DIM), lambda b, h, i: (ka + h, b, 0)),
            pl.BlockSpec((nh, seq, HEAD_DIM), lambda b, h, i: (va + h, b, 0)),
            pl.BlockSpec((t, t), lambda b, h, i: (0, 0)),
        ],
        out_specs=pl.BlockSpec((t, nh * HEAD_DIM), lambda b, h, i: (b * nq + i, h)),
        compiler_params=_cparams(3),
        name="sb_attn",
    )(proj, proj, proj, tri)


def _sortable(score):
    u = lax.bitcast_convert_type(score, jnp.int32)
    return u ^ (lax.shift_right_arithmetic(u, 31) & 0x7FFFFFFF)


def _dsa_kernel(qi_ref, ke_ref, wi_ref, qb_ref, kb_ref, vb_ref, o_ref,
                wt_ref, vt_ref, score_ref, key_ref, k16_ref, bias_ref, *, tq, seq, hi, hd, topk,
                nbuckets):
    i = pl.program_id(1)
    npairs = hi // 2
    tchunk = 256 if seq % 256 == 0 else LANES

    @pl.when(i == 0)
    def _():
        def xpose(h, c):
            for r in range(seq // tchunk):
                blk = vb_ref[h, r * tchunk:(r + 1) * tchunk, :].astype(_F32)
                vt_ref[h, :, r * tchunk:(r + 1) * tchunk] = blk.T.astype(_BF16)
            return c
        lax.fori_loop(0, hd, xpose, 0)

    wt_ref[...] = wi_ref[...].T

    def run(width):
        key_pos = lax.broadcasted_iota(jnp.int32, (width, tq), 0)
        q_pos = i * tq + lax.broadcasted_iota(jnp.int32, (width, tq), 1)
        visible = key_pos <= q_pos

        score_ref[:width, :] = jnp.zeros((width, tq), _F32)

        def pair(p, c):
            qp = qi_ref[p]
            d_even = lax.dot_general(ke_ref[0, :width, :], qp, _NT, preferred_element_type=_F32)
            d_odd = lax.dot_general(ke_ref[1, :width, :], qp, _NT, preferred_element_type=_F32)
            w_even = wt_ref[pl.ds(2 * p, 1), :]
            w_odd = wt_ref[pl.ds(2 * p + 1, 1), :]
            score_ref[:width, :] += (w_even * jnp.maximum(d_even, 0.0)
                                     + w_odd * jnp.maximum(d_odd, 0.0))
            return c

        lax.fori_loop(0, npairs, pair, 0)
        key_ref[:width, :] = jnp.where(visible, _sortable(score_ref[:width, :]), INT_MIN)

        def count_ge16(cand):
            hit = jnp.where(k16_ref[:width, :] >= cand.astype(jnp.int16), jnp.int16(1), jnp.int16(0))
            parts = [hit[r * PACK16:(r + 1) * PACK16, :] for r in range(width // PACK16)]
            while len(parts) > 1:
                odd = parts[-1:] if len(parts) % 2 else []
                parts = [a + b for a, b in zip(parts[0::2], parts[1::2])] + odd
            return jnp.sum(parts[0].astype(jnp.int32), axis=0, keepdims=True)

        def search16(need):
            prefix = jnp.where(count_ge16(jnp.zeros((1, tq), jnp.int32)) >= need, 0, I16_MIN)

            def step(b, prefix):
                cand = prefix + lax.shift_left(jnp.int32(1), 14 - b)
                return jnp.where(count_ge16(cand) >= need, cand, prefix)

            return lax.fori_loop(0, 15, step, prefix.astype(jnp.int32))

        key_hi = lax.shift_right_arithmetic(key_ref[:width, :], 16)
        k16_ref[:width, :] = key_hi.astype(jnp.int16)
        t_hi = search16(jnp.full((1, tq), topk, jnp.int32))
        n_above = jnp.where(t_hi < I16_MAX, count_ge16(jnp.minimum(t_hi + 1, I16_MAX)), 0)
        key = key_ref[:width, :]
        key_lo = (key & 0xFFFF) + I16_MIN
        tied = lax.shift_right_arithmetic(key, 16) == t_hi
        k16_ref[:width, :] = jnp.where(tied, key_lo, I16_MIN).astype(jnp.int16)
        t_lo = search16(topk - n_above)
        thr = t_hi * 65536 + (t_lo - I16_MIN)
        thr = jnp.maximum(thr, INT_MIN + 1)
        bias_ref[:width, :] = jnp.where(key_ref[:width, :] >= thr, 0.0, NEG_BIG)

        def head(h, c):
            s = lax.dot_general(kb_ref[h, :width, :], qb_ref[h], _NT, preferred_element_type=_F32)
            s = s + bias_ref[:width, :]
            mx = jnp.max(s, axis=0, keepdims=True)
            p = jnp.exp(s - mx)
            den = jnp.sum(p, axis=0, keepdims=True)
            ot = jnp.dot(vt_ref[h, :, :width], p.astype(_BF16), preferred_element_type=_F32)
            o_ref[h] = (ot / den).T.astype(_BF16)
            return c

        lax.fori_loop(0, hd, head, 0)

    nq = seq // tq
    per = nq // nbuckets
    bucket = i // per
    for bi in range(nbuckets):
        pl.when(bucket == bi)(functools.partial(run, (bi + 1) * per * tq))


def _dsa_attention(proj, wi, lay, batch, seq, hi, hd, tq):
    m = batch * seq
    nq = seq // tq
    topk = min(TOPK_MAX, seq // 4)
    nbuckets = 4 if nq % 4 == 0 else 1
    npairs = hi // 2
    qi_b = lay["qi"][0] // npairs
    ke_b = lay["misc"][0] // 2
    qb_b, kb_b, vb_b = (lay[n][0] // hd for n in ("qb", "kb", "vb"))
    assert lay["qi"][0] % npairs == 0 and lay["misc"][0] % 2 == 0
    assert all(lay[n][0] % hd == 0 for n in ("qb", "kb", "vb"))
    kern = functools.partial(_dsa_kernel, tq=tq, seq=seq, hi=hi, hd=hd, topk=topk,
                             nbuckets=nbuckets)
    return pl.pallas_call(
        kern,
        out_shape=jax.ShapeDtypeStruct((hd, m, HEAD_DIM), _BF16),
        grid=(batch, nq),
        in_specs=[
            pl.BlockSpec((npairs, tq, LANES), lambda b, i: (qi_b, b * nq + i, 0)),
            pl.BlockSpec((2, seq, LANES), lambda b, i: (ke_b, b, 0)),
            pl.BlockSpec((tq, LANES), lambda b, i: (b * nq + i, 0)),
            pl.BlockSpec((hd, tq, HEAD_DIM), lambda b, i: (qb_b, b * nq + i, 0)),
            pl.BlockSpec((hd, seq, HEAD_DIM), lambda b, i: (kb_b, b, 0)),
            pl.BlockSpec((hd, seq, HEAD_DIM), lambda b, i: (vb_b, b, 0)),
        ],
        out_specs=pl.BlockSpec((hd, tq, HEAD_DIM), lambda b, i: (0, b * nq + i, 0)),
        scratch_shapes=[pltpu.VMEM((LANES, tq), _F32),
                        pltpu.VMEM((hd, HEAD_DIM, seq), _BF16),
                        pltpu.VMEM((seq, tq), _F32),
                        pltpu.VMEM((seq, tq), jnp.int32),
                        pltpu.VMEM((seq, tq), jnp.int16),
                        pltpu.VMEM((seq, tq), _F32)],
        compiler_params=_cparams(2),
        name="dsa_attn",
    )(proj, proj, wi, proj, proj, proj)


def _merge_kernel(ya_ref, yb_ref, wa_ref, wb_ref, ga_ref, gb_ref, o_ref, *, hd):
    yb = jnp.concatenate([yb_ref[h] for h in range(hd)], axis=1)
    a = jnp.dot(ya_ref[...], wa_ref[...], preferred_element_type=_F32)
    b = jnp.dot(yb, wb_ref[...], preferred_element_type=_F32)
    for c in range(CHUNKS_PER_BLOCK):
        sl = slice(c * LANES, (c + 1) * LANES)
        o_ref[:, sl] = (ga_ref[c].astype(_F32) * a[:, sl]
                        + gb_ref[c].astype(_F32) * b[:, sl]).astype(_BF16)


def _merge(ya, yb, wa, wb, proj, lay, tm):
    m, ka = ya.shape
    hd = yb.shape[0]
    d = wa.shape[1]
    ga_b = lay["ga"][0] // CHUNKS_PER_BLOCK
    gb_b = lay["gb"][0] // CHUNKS_PER_BLOCK
    gate = lambda off: pl.BlockSpec((CHUNKS_PER_BLOCK, tm, LANES), lambda i, j: (off + j, i, 0))
    return pl.pallas_call(
        functools.partial(_merge_kernel, hd=hd),
        out_shape=jax.ShapeDtypeStruct((m, d), _BF16),
        grid=(m // tm, d // PROJ_TN),
        in_specs=[
            pl.BlockSpec((tm, ka), lambda i, j: (i, 0)),
            pl.BlockSpec((hd, tm, HEAD_DIM), lambda i, j: (0, i, 0)),
            pl.BlockSpec((ka, PROJ_TN), lambda i, j: (0, j)),
            pl.BlockSpec((hd * HEAD_DIM, PROJ_TN), lambda i, j: (0, j)),
            gate(ga_b), gate(gb_b),
        ],
        out_specs=pl.BlockSpec((tm, PROJ_TN), lambda i, j: (i, j)),
        compiler_params=_cparams(2),
        name="merge",
    )(ya, yb, wa, wb, proj, proj)


def _out_res_kernel(a_ref, w_ref, x_ref, gt_ref, o_ref):
    acc = jnp.dot(a_ref[...], w_ref[...], preferred_element_type=_F32)
    o_ref[...] = x_ref[...] + (1.0 + gt_ref[...]) * acc


def _out_res(a, w, x2, gt, seq, tm, tn):
    m, k = a.shape
    d = w.shape[1]
    tps = seq // tm
    return pl.pallas_call(
        _out_res_kernel,
        out_shape=jax.ShapeDtypeStruct((m, d), _F32),
        grid=(m // tm, d // tn),
        in_specs=[
            pl.BlockSpec((tm, k), lambda i, j: (i, 0)),
            pl.BlockSpec((k, tn), lambda i, j: (0, j)),
            pl.BlockSpec((tm, tn), lambda i, j: (i, j)),
            pl.BlockSpec((None, 1, tn), lambda i, j: (i // tps, 0, j)),
        ],
        out_specs=pl.BlockSpec((tm, tn), lambda i, j: (i, j)),
        compiler_params=_cparams(2),
        name="out_res",
    )(a, w, x2, gt)


def _ffn_up_kernel(x_ref, g_ref, sc_ref, sh_ref, wa_ref, wb_ref, cwa_ref, cwb_ref, cba_ref,
                   cbb_ref, o_ref, h_ref, u_ref, carry_ref, *, tm, tn, tps):
    i = pl.program_id(0)
    j = pl.program_id(1)
    pad = SUBLANES
    rc = min(ROW_CHUNK, tm)

    @pl.when(j == 0)
    def _():
        h = _norm_modulate(x_ref[...], g_ref[...], sc_ref[...], sh_ref[...])
        h_ref[...] = h.astype(_BF16)

    @pl.when(i % tps == 0)
    def _():
        u_ref[0:pad, :] = jnp.zeros((pad, 2 * tn), _F32)

    @pl.when(i % tps != 0)
    def _():
        u_ref[0:pad, :] = carry_ref[j]

    def conv(u, cols, lo, cw_ref, cb_ref):
        cw = cw_ref[...]
        u1 = u_ref[lo + pad - 1:lo + pad - 1 + rc, cols]
        u2 = u_ref[lo + pad - 2:lo + pad - 2 + rc, cols]
        return cb_ref[...] + (cw[0:1, :] * u2 + cw[1:2, :] * u1 + cw[2:3, :] * u)

    for r in range(tm // rc):
        lo = r * rc
        hr = h_ref[lo:lo + rc, :]
        ua = jnp.dot(hr, wa_ref[...], preferred_element_type=_F32)
        ub = jnp.dot(hr, wb_ref[...], preferred_element_type=_F32)
        u_ref[lo + pad:lo + pad + rc, 0:tn] = ua
        u_ref[lo + pad:lo + pad + rc, tn:2 * tn] = ub
        a = conv(ua, slice(0, tn), lo, cwa_ref, cba_ref)
        b = conv(ub, slice(tn, 2 * tn), lo, cwb_ref, cbb_ref)
        o_ref[lo:lo + rc, :] = (a * jax.nn.sigmoid(a) * b).astype(_BF16)

    carry_ref[j] = u_ref[tm:tm + pad, :]


def _ffn_up(x2, g, sc, sh, w, cw, cb, seq, tm, tn):
    m, d = x2.shape
    nblk = w.shape[1] // (2 * tn)
    tps = seq // tm
    mod_spec = pl.BlockSpec((None, 1, d), lambda i, j: (i // tps, 0, 0))
    halves = lambda rows: (pl.BlockSpec((rows, tn), lambda i, j: (0, j)),
                           pl.BlockSpec((rows, tn), lambda i, j: (0, nblk + j)))
    return pl.pallas_call(
        functools.partial(_ffn_up_kernel, tm=tm, tn=tn, tps=tps),
        out_shape=jax.ShapeDtypeStruct((m, nblk * tn), _BF16),
        grid=(m // tm, nblk),
        in_specs=[
            pl.BlockSpec((tm, d), lambda i, j: (i, 0)),
            pl.BlockSpec((1, d), lambda i, j: (0, 0)),
            mod_spec, mod_spec,
            *halves(d), *halves(CONV_WIDTH), *halves(1),
        ],
        out_specs=pl.BlockSpec((tm, tn), lambda i, j: (i, j)),
        scratch_shapes=[pltpu.VMEM((tm, d), _BF16),
                        pltpu.VMEM((tm + SUBLANES, 2 * tn), _F32),
                        pltpu.VMEM((nblk, SUBLANES, 2 * tn), _F32)],
        compiler_params=_cparams(2),
        name="ffn_up",
    )(x2, g, sc, sh, w, w, cw, cw, cb, cb)


def _final_norm_kernel(x_ref, g_ref, o_ref):
    x = x_ref[...]
    ms = jnp.mean(x * x, axis=-1, keepdims=True)
    o_ref[...] = (x * lax.rsqrt(ms + NORM_EPS)) * g_ref[...]


def _final_norm(x2, g, tm):
    m, d = x2.shape
    return pl.pallas_call(
        _final_norm_kernel,
        out_shape=jax.ShapeDtypeStruct((m, d), _F32),
        grid=(m // tm,),
        in_specs=[pl.BlockSpec((tm, d), lambda i: (i, 0)),
                  pl.BlockSpec((1, d), lambda i: (0, 0))],
        out_specs=pl.BlockSpec((tm, d), lambda i: (i, 0)),
        compiler_params=_cparams(1),
        name="final_norm",
    )(x2, g)


def _arrange_w_in(w_in, d, hs, hd, hi):
    sw, dw, iw = hs * HEAD_DIM, hd * HEAD_DIM, hi * IDX_DIM
    sizes = (sw, sw, sw, dw, dw, dw, iw, IDX_DIM, hi, d, d)
    offs = [0]
    for s in sizes:
        offs.append(offs[-1] + s)
    qa, ka, va, qb, kb, vb, qi, ki, wi, ga, gb = (
        w_in[..., offs[n]:offs[n + 1]] for n in range(len(sizes)))
    z = lambda n: jnp.zeros(w_in.shape[:-1] + (n,), w_in.dtype)
    misc = jnp.concatenate([ki, z(LANES - IDX_DIM), z(LANES - IDX_DIM), ki,
                            wi, z(LANES - hi), z(LANES)], axis=-1)
    return jnp.concatenate([ga, gb, qa, ka, va, qb, kb, vb, qi, misc], axis=-1).astype(_BF16)


def _rope_tables(positions):
    pos = positions.reshape(-1).astype(_F32)

    def cos_sin(dim):
        inv_freq = 1.0 / (ROPE_THETA ** (jnp.arange(0, dim, 2, dtype=_F32) / dim))
        ang = pos[:, None] * inv_freq
        return jnp.cos(ang), jnp.sin(ang)

    c, s = cos_sin(HEAD_DIM)
    c128 = jnp.concatenate([c, c], axis=-1)
    s128 = jnp.concatenate([-s, s], axis=-1)
    c, s = cos_sin(IDX_DIM)
    zero = jnp.zeros_like(s)
    c64 = jnp.concatenate([c, c, c, c], axis=-1)
    s64lo = jnp.concatenate([-s, zero, -s, zero], axis=-1)
    s64hi = jnp.concatenate([zero, s, zero, s], axis=-1)
    return c128, s128, c64, s64lo, s64hi


def _pick(n, prefs):
    for p in prefs:
        if n % p == 0:
            return p
    return n


def kernel(x, c, positions, w_in, w_a, w_b, w_o, w_ada, b_ada, g_mix, g_ffn, w_up, conv_w,
           conv_b, w_down, g_final):
    batch, seq, d = x.shape
    depth = w_in.shape[0]
    hs = w_a.shape[1] // HEAD_DIM
    hd = w_b.shape[1] // HEAD_DIM
    rest = w_in.shape[2] - 3 * hs * HEAD_DIM - 3 * hd * HEAD_DIM - 2 * d - IDX_DIM
    hi = rest // (IDX_DIM + 1)
    dff = w_down.shape[1]
    m = batch * seq
    lay = _proj_layout(d, hs, hd, hi)

    tm = _pick(seq, (1024, 512, 256, 128))
    t_sb = _pick(seq, (256, 128))
    nh_sb = 2
    tq_dsa = _pick(seq, (256, 128))
    tn_ffn = _pick(dff, (512, 256, 128))
    tn_out = _pick(d, (512, 256, 128))

    w_in_r = _arrange_w_in(w_in, d, hs, hd, hi)
    w_a16, w_b16, w_o16, w_up16, w_down16 = (t.astype(_BF16)
                                             for t in (w_a, w_b, w_o, w_up, w_down))
    tabs = _rope_tables(positions)
    mod = _modulation(c, w_ada, b_ada).reshape(depth, batch, N_MOD, 1, d)

    x2 = x.reshape(m, d)
    for l in range(depth):
        sh1, sc1, gt1, sh2, sc2, gt2 = (mod[l, :, n] for n in range(N_MOD))
        proj, wi = _in_proj(x2, g_mix[l].reshape(1, d), sc1, sh1, w_in_r[l], tabs, lay, hi,
                            seq, tm)
        ya = _sb_attention(proj, lay, batch, seq, hs, t_sb, nh_sb)
        yb = _dsa_attention(proj, wi, lay, batch, seq, hi, hd, tq_dsa)
        merged = _merge(ya, yb, w_a16[l], w_b16[l], proj, lay, tm)
        x2 = _out_res(merged, w_o16[l], x2, gt1, seq, tm, tn_out)
        g = _ffn_up(x2, g_ffn[l].reshape(1, d), sc2, sh2, w_up16[l], conv_w[l],
                    conv_b[l].reshape(1, -1), seq, tm, tn_ffn)
        x2 = _out_res(g, w_down16[l], x2, gt2, seq, tm, tn_out)
    out = _final_norm(x2, g_final.reshape(1, d), tm)
    return out.reshape(batch, seq, d)
```

```python
import functools
import math

import jax
import jax.numpy as jnp
from jax import lax
from jax.experimental import pallas as pl
from jax.experimental.pallas import tpu as pltpu

HEAD_DIM = 128
IDX_DIM = 64
TOPK_MAX = 256
CONV_WIDTH = 3
ROPE_THETA = 10000.0
NORM_EPS = 1e-6
N_MOD = 6
LANES = 128
SUBLANES = 8
CHUNKS_PER_BLOCK = 4
PROJ_TN = LANES * CHUNKS_PER_BLOCK
VMEM_LIMIT = 56 * 1024 * 1024
INT_MIN = -2 ** 31
I16_MIN, I16_MAX = -2 ** 15, 2 ** 15 - 1
PACK16 = 16
NEG_BIG = -1e30
ROW_CHUNK = 256
LOG2E = math.log2(math.e)

_F32 = jnp.float32
_BF16 = jnp.bfloat16
_NT = (((1,), (1,)), ((), ()))


def _cparams(n_axes):
    return pltpu.CompilerParams(dimension_semantics=("arbitrary",) * n_axes,
                                vmem_limit_bytes=VMEM_LIMIT)


def _mod_kernel(c_ref, w_ref, b_ref, o_ref):
    c = c_ref[...]
    c_act = (c * jax.nn.sigmoid(c)).astype(_BF16)
    acc = jnp.dot(c_act, w_ref[...].astype(_BF16), preferred_element_type=_F32)
    o_ref[...] = acc + b_ref[...]


def _modulation(c, w_ada, b_ada):
    depth, d, n = w_ada.shape
    b = c.shape[0]
    tn = 1024 if n % 1024 == 0 else n
    return pl.pallas_call(
        _mod_kernel,
        out_shape=jax.ShapeDtypeStruct((depth, b, n), _F32),
        grid=(depth, n // tn),
        in_specs=[
            pl.BlockSpec((b, d), lambda l, j: (0, 0)),
            pl.BlockSpec((None, d, tn), lambda l, j: (l, 0, j)),
            pl.BlockSpec((None, 1, tn), lambda l, j: (l, 0, j)),
        ],
        out_specs=pl.BlockSpec((None, b, tn), lambda l, j: (l, 0, j)),
        compiler_params=_cparams(2),
        name="adaln_mod",
    )(c, w_ada, b_ada.reshape(depth, 1, n))


def _norm_modulate(x, g, sc, sh):
    ms = jnp.mean(x * x, axis=-1, keepdims=True)
    r = x * lax.rsqrt(ms + NORM_EPS)
    return (r * g) * (1.0 + sc) + sh


def _proj_layout(d, hs, hd, hi):
    sizes = [("ga", d // LANES), ("gb", d // LANES),
             ("qa", hs), ("ka", hs), ("va", hs),
             ("qb", hd), ("kb", hd), ("vb", hd),
             ("qi", hi * IDX_DIM // LANES), ("misc", CHUNKS_PER_BLOCK)]
    off, lay = 0, {}
    for name, n in sizes:
        assert n % CHUNKS_PER_BLOCK == 0, (name, n)
        lay[name] = (off, n)
        off += n
    lay["total"] = off
    return lay


def _rope128(t, cos, sin):
    return t * cos + pltpu.roll(t, HEAD_DIM // 2, axis=1) * sin


def _rope64(t, cos, sin_lo, sin_hi):
    half = IDX_DIM // 2
    return (t * cos + pltpu.roll(t, LANES - half, axis=1) * sin_lo
            + pltpu.roll(t, half, axis=1) * sin_hi)


def _in_proj_kernel(x_ref, g_ref, sc_ref, sh_ref, w_ref, c128_ref, s128_ref,
                    c64_ref, s64lo_ref, s64hi_ref, o_ref, wi_ref, h_ref, *, lay, idx_scale, tm):
    j = pl.program_id(1)

    @pl.when(j == 0)
    def _():
        h = _norm_modulate(x_ref[...], g_ref[...], sc_ref[...], sh_ref[...])
        h_ref[...] = h.astype(_BF16)

    blk = lambda name: (lay[name][0] // CHUNKS_PER_BLOCK,
                        (lay[name][0] + lay[name][1]) // CHUNKS_PER_BLOCK)
    qa_scale = (HEAD_DIM ** -0.5) * LOG2E
    qb_scale = HEAD_DIM ** -0.5
    rc = min(ROW_CHUNK, tm)

    def run(epilogue):
        for r in range(tm // rc):
            rows = slice(r * rc, (r + 1) * rc)
            acc = jnp.dot(h_ref[rows, :], w_ref[...], preferred_element_type=_F32)
            epilogue(acc, rows)

    def per_chunk(fn):
        def epilogue(acc, rows):
            for c in range(CHUNKS_PER_BLOCK):
                o_ref[c, rows, :] = fn(acc[:, c * LANES:(c + 1) * LANES], rows).astype(_BF16)
        return epilogue

    def in_group(*names):
        cond = None
        for nm in names:
            lo, hi = blk(nm)
            t = jnp.logical_and(j >= lo, j < hi)
            cond = t if cond is None else jnp.logical_or(cond, t)
        return cond

    rope128 = lambda t, rows: _rope128(t, c128_ref[rows, :], s128_ref[rows, :])
    rope64 = lambda t, rows: _rope64(t, c64_ref[rows, :], s64lo_ref[rows, :], s64hi_ref[rows, :])

    @pl.when(in_group("ga", "gb"))
    def _():
        run(per_chunk(lambda t, rows: jax.nn.sigmoid(t)))

    @pl.when(in_group("qa"))
    def _():
        run(per_chunk(lambda t, rows: t * qa_scale))

    @pl.when(in_group("ka", "va", "vb"))
    def _():
        run(per_chunk(lambda t, rows: t))

    @pl.when(in_group("qb"))
    def _():
        run(per_chunk(lambda t, rows: rope128(t, rows) * qb_scale))

    @pl.when(in_group("kb"))
    def _():
        run(per_chunk(rope128))

    @pl.when(in_group("qi"))
    def _():
        run(per_chunk(rope64))

    @pl.when(in_group("misc"))
    def _():
        def epilogue(acc, rows):
            for c in range(2):
                o_ref[c, rows, :] = rope64(acc[:, c * LANES:(c + 1) * LANES], rows).astype(_BF16)
            o_ref[2, rows, :] = acc[:, 2 * LANES:3 * LANES].astype(_BF16)
            o_ref[3, rows, :] = acc[:, 3 * LANES:4 * LANES].astype(_BF16)
            wi_ref[rows, :] = acc[:, 2 * LANES:3 * LANES] * idx_scale
        run(epilogue)


def _in_proj(x2, g, sc, sh, w, layer, tabs, lay, hi, seq, tm):
    m, d = x2.shape
    nblk = lay["total"] // CHUNKS_PER_BLOCK
    tps = seq // tm
    row_spec = pl.BlockSpec((tm, LANES), lambda i, j: (i, 0))
    mod_spec = pl.BlockSpec((None, 1, d), lambda i, j: (i // tps, 0, 0))
    kern = functools.partial(_in_proj_kernel, lay=lay, tm=tm,
                             idx_scale=(IDX_DIM ** -0.5) * (hi ** -0.5))
    return pl.pallas_call(
        kern,
        out_shape=(jax.ShapeDtypeStruct((lay["total"], m, LANES), _BF16),
                   jax.ShapeDtypeStruct((m, LANES), _F32)),
        grid=(m // tm, nblk),
        in_specs=[
            pl.BlockSpec((tm, d), lambda i, j: (i, 0)),
            pl.BlockSpec((None, 1, d), lambda i, j: (layer, 0, 0)),
            mod_spec, mod_spec,
            pl.BlockSpec((None, d, PROJ_TN), lambda i, j: (layer, 0, j)),
            row_spec, row_spec, row_spec, row_spec, row_spec,
        ],
        out_specs=(pl.BlockSpec((CHUNKS_PER_BLOCK, tm, LANES), lambda i, j: (j, i, 0)),
                   pl.BlockSpec((tm, LANES), lambda i, j: (i, 0))),
        scratch_shapes=[pltpu.VMEM((tm, d), _BF16)],
        compiler_params=_cparams(2),
        name="in_proj",
    )(x2, g, sc, sh, w, *tabs)


def _add_lane_replicated(x, rep):
    n = x.shape[1] // LANES
    return jnp.concatenate([x[:, c * LANES:(c + 1) * LANES] + rep for c in range(n)], axis=1)


def _sb_kernel(q_ref, k_ref, v_ref, tri_ref, o_ref, *, t, nh, nq):
    qi = pl.program_id(2)

    def variant(n):
        past = (lax.broadcasted_iota(jnp.int32, (t, t), 1)
                < lax.broadcasted_iota(jnp.int32, (t, t), 0))
        for hh in range(nh):
            q = q_ref[hh]
            acc = jnp.zeros((t, HEAD_DIM), _F32)
            carry = jnp.zeros((t, LANES), _F32)
            for kb in range(n, -1, -1):
                diag = kb == n
                k = k_ref[hh, kb * t:(kb + 1) * t, :]
                v = v_ref[hh, kb * t:(kb + 1) * t, :]
                z = lax.dot_general(q, k, _NT, preferred_element_type=_F32)
                sp = jnp.maximum(z, 0.0) + jnp.log(1.0 + jnp.exp2(-jnp.abs(z))) * LOG2E
                log1m = -sp
                if diag:
                    log1m = jnp.where(past, log1m, 0.0)
                l1m = log1m.astype(_BF16)
                cum = jnp.dot(l1m, tri_ref[...], preferred_element_type=_F32)
                w = jnp.exp2(z - sp + _add_lane_replicated(cum, carry))
                if diag:
                    w = jnp.where(past, w, 0.0)
                acc = acc + jnp.dot(w.astype(_BF16), v, preferred_element_type=_F32)
                if kb > 0:
                    total = cum[:, 0:1] + l1m[:, 0:1].astype(_F32)
                    carry = carry + jnp.broadcast_to(total, (t, LANES))
            o_ref[:, hh * HEAD_DIM:(hh + 1) * HEAD_DIM] = acc.astype(_BF16)

    for n in range(nq):
        pl.when(qi == n)(functools.partial(variant, n))


def _sb_attention(proj, lay, batch, seq, hs, t, nh):
    m = batch * seq
    nq = seq // t
    qa, ka, va = (lay[n][0] // nh for n in ("qa", "ka", "va"))
    assert hs % nh == 0 and all(lay[n][0] % nh == 0 for n in ("qa", "ka", "va"))
    r = lax.broadcasted_iota(jnp.int32, (t, t), 0)
    c = lax.broadcasted_iota(jnp.int32, (t, t), 1)
    tri = jnp.where(r > c, 1.0, 0.0).astype(_BF16)
    return pl.pallas_call(
        functools.partial(_sb_kernel, t=t, nh=nh, nq=nq),
        out_shape=jax.ShapeDtypeStruct((m, hs * HEAD_DIM), _BF16),
        grid=(batch, hs // nh, nq),
        in_specs=[
            pl.BlockSpec((nh, t, HEAD_DIM), lambda b, h, i: (qa + h, b * nq + i, 0)),
            pl.BlockSpec((nh, seq, HEAD_DIM), lambda b, h, i: (ka + h, b, 0)),
            pl.BlockSpec((nh, seq, HEAD_DIM), lambda b, h, i: (va + h, b, 0)),
            pl.BlockSpec((t, t), lambda b, h, i: (0, 0)),
        ],
        out_specs=pl.BlockSpec((t, nh * HEAD_DIM), lambda b, h, i: (b * nq + i, h)),
        compiler_params=_cparams(3),
        name="sb_attn",
    )(proj, proj, proj, tri)


def _sortable(score):
    u = lax.bitcast_convert_type(score, jnp.int32)
    return u ^ (lax.shift_right_arithmetic(u, 31) & 0x7FFFFFFF)


def _dsa_kernel(qi_ref, ke_ref, wi_ref, qb_ref, kb_ref, vb_ref, o_ref,
                wt_ref, vt_ref, score_ref, key_ref, k16_ref, bias_ref, *, tq, seq, hi, hd, topk,
                nbuckets):
    i = pl.program_id(1)
    npairs = hi // 2
    tchunk = 256 if seq % 256 == 0 else LANES

    @pl.when(i == 0)
    def _():
        def xpose(h, c):
            for r in range(seq // tchunk):
                blk = vb_ref[h, r * tchunk:(r + 1) * tchunk, :].astype(_F32)
                vt_ref[h, :, r * tchunk:(r + 1) * tchunk] = blk.T.astype(_BF16)
            return c
        lax.fori_loop(0, hd, xpose, 0)

    wt_ref[...] = wi_ref[...].T

    def run(width):
        key_pos = lax.broadcasted_iota(jnp.int32, (width, tq), 0)
        q_pos = i * tq + lax.broadcasted_iota(jnp.int32, (width, tq), 1)
        visible = key_pos <= q_pos

        def pair(p):
            qp = qi_ref[p]
            d_even = lax.dot_general(ke_ref[0, :width, :], qp, _NT, preferred_element_type=_F32)
            d_odd = lax.dot_general(ke_ref[1, :width, :], qp, _NT, preferred_element_type=_F32)
            return (wt_ref[2 * p:2 * p + 1, :] * jnp.maximum(d_even, 0.0)
                    + wt_ref[2 * p + 1:2 * p + 2, :] * jnp.maximum(d_odd, 0.0))

        score_ref[:width, :] = pair(0)
        for p in range(1, npairs):
            score_ref[:width, :] += pair(p)
        key_ref[:width, :] = jnp.where(visible, _sortable(score_ref[:width, :]), INT_MIN)

        def count_ge16(cand):
            hit = jnp.where(k16_ref[:width, :] >= cand.astype(jnp.int16), jnp.int16(1), jnp.int16(0))
            parts = [hit[r * PACK16:(r + 1) * PACK16, :] for r in range(width // PACK16)]
            while len(parts) > 1:
                odd = parts[-1:] if len(parts) % 2 else []
                parts = [a + b for a, b in zip(parts[0::2], parts[1::2])] + odd
            return jnp.sum(parts[0].astype(jnp.int32), axis=0, keepdims=True)

        def search16(need):
            prefix = jnp.where(count_ge16(jnp.zeros((1, tq), jnp.int32)) >= need, 0, I16_MIN)

            def step(b, prefix):
                cand = prefix + lax.shift_left(jnp.int32(1), 14 - b)
                return jnp.where(count_ge16(cand) >= need, cand, prefix)

            return lax.fori_loop(0, 15, step, prefix.astype(jnp.int32))

        key_hi = lax.shift_right_arithmetic(key_ref[:width, :], 16)
        k16_ref[:width, :] = key_hi.astype(jnp.int16)
        t_hi = search16(jnp.full((1, tq), topk, jnp.int32))
        n_above = jnp.where(t_hi < I16_MAX, count_ge16(jnp.minimum(t_hi + 1, I16_MAX)), 0)
        key = key_ref[:width, :]
        key_lo = (key & 0xFFFF) + I16_MIN
        tied = lax.shift_right_arithmetic(key, 16) == t_hi
        k16_ref[:width, :] = jnp.where(tied, key_lo, I16_MIN).astype(jnp.int16)
        t_lo = search16(topk - n_above)
        thr = t_hi * 65536 + (t_lo - I16_MIN)
        thr = jnp.maximum(thr, INT_MIN + 1)
        bias_ref[:width, :] = jnp.where(key_ref[:width, :] >= thr, 0.0, NEG_BIG)

        def logits(h):
            s = lax.dot_general(kb_ref[h, :width, :], qb_ref[h], _NT, preferred_element_type=_F32)
            return s + bias_ref[:width, :]

        def finish(h, s):
            mx = jnp.max(s, axis=0, keepdims=True)
            p = jnp.exp(s - mx)
            den = jnp.sum(p, axis=0, keepdims=True)
            ot = jnp.dot(vt_ref[h, :, :width], p.astype(_BF16), preferred_element_type=_F32)
            o_ref[h] = (ot / den).T.astype(_BF16)

        s_next = logits(0)
        for h in range(hd):
            s_cur = s_next
            if h + 1 < hd:
                s_next = logits(h + 1)
            finish(h, s_cur)

    nq = seq // tq
    per = nq // nbuckets
    bucket = i // per
    for bi in range(nbuckets):
        pl.when(bucket == bi)(functools.partial(run, (bi + 1) * per * tq))


def _dsa_attention(proj, wi, lay, batch, seq, hi, hd, tq):
    m = batch * seq
    nq = seq // tq
    topk = min(TOPK_MAX, seq // 4)
    nbuckets = 4 if nq % 4 == 0 else 1
    npairs = hi // 2
    qi_b = lay["qi"][0] // npairs
    ke_b = lay["misc"][0] // 2
    qb_b, kb_b, vb_b = (lay[n][0] // hd for n in ("qb", "kb", "vb"))
    assert lay["qi"][0] % npairs == 0 and lay["misc"][0] % 2 == 0
    assert all(lay[n][0] % hd == 0 for n in ("qb", "kb", "vb"))
    kern = functools.partial(_dsa_kernel, tq=tq, seq=seq, hi=hi, hd=hd, topk=topk,
                             nbuckets=nbuckets)
    return pl.pallas_call(
        kern,
        out_shape=jax.ShapeDtypeStruct((hd, m, HEAD_DIM), _BF16),
        grid=(batch, nq),
        in_specs=[
            pl.BlockSpec((npairs, tq, LANES), lambda b, i: (qi_b, b * nq + i, 0)),
            pl.BlockSpec((2, seq, LANES), lambda b, i: (ke_b, b, 0)),
            pl.BlockSpec((tq, LANES), lambda b, i: (b * nq + i, 0)),
            pl.BlockSpec((hd, tq, HEAD_DIM), lambda b, i: (qb_b, b * nq + i, 0)),
            pl.BlockSpec((hd, seq, HEAD_DIM), lambda b, i: (kb_b, b, 0)),
            pl.BlockSpec((hd, seq, HEAD_DIM), lambda b, i: (vb_b, b, 0)),
        ],
        out_specs=pl.BlockSpec((hd, tq, HEAD_DIM), lambda b, i: (0, b * nq + i, 0)),
        scratch_shapes=[pltpu.VMEM((LANES, tq), _F32),
                        pltpu.VMEM((hd, HEAD_DIM, seq), _BF16),
                        pltpu.VMEM((seq, tq), _F32),
                        pltpu.VMEM((seq, tq), jnp.int32),
                        pltpu.VMEM((seq, tq), jnp.int16),
                        pltpu.VMEM((seq, tq), _F32)],
        compiler_params=_cparams(2),
        name="dsa_attn",
    )(proj, proj, wi, proj, proj, proj)


def _mix_out_kernel(ya_ref, yb_ref, wa_ref, wb_ref, ga_ref, gb_ref, wo_ref, x_ref, gt_ref,
                    o_ref, mg_ref, *, hd, tm):
    rc = min(ROW_CHUNK, tm)
    for r in range(tm // rc):
        rows = slice(r * rc, (r + 1) * rc)
        yb = jnp.concatenate([yb_ref[h, rows, :] for h in range(hd)], axis=1)
        a = jnp.dot(ya_ref[rows, :], wa_ref[...], preferred_element_type=_F32)
        b = jnp.dot(yb, wb_ref[...], preferred_element_type=_F32)
        for c in range(a.shape[1] // LANES):
            sl = slice(c * LANES, (c + 1) * LANES)
            mg_ref[rows, sl] = (ga_ref[c, rows, :].astype(_F32) * a[:, sl]
                                + gb_ref[c, rows, :].astype(_F32) * b[:, sl]).astype(_BF16)
    for r in range(tm // rc):
        rows = slice(r * rc, (r + 1) * rc)
        acc = jnp.dot(mg_ref[rows, :], wo_ref[...], preferred_element_type=_F32)
        o_ref[rows, :] = x_ref[rows, :] + (1.0 + gt_ref[...]) * acc


def _mix_out(ya, yb, wa, wb, wo, layer, proj, x2, gt, lay, seq, tm):
    m, ka = ya.shape
    hd = yb.shape[0]
    d = wo.shape[2]
    nchunks = d // LANES
    tps = seq // tm
    assert lay["ga"][0] % nchunks == 0 and lay["gb"][0] % nchunks == 0
    gate = lambda name: pl.BlockSpec((nchunks, tm, LANES),
                                     lambda i: (lay[name][0] // nchunks, i, 0))
    resident = lambda rows: pl.BlockSpec((None, rows, d), lambda i: (layer, 0, 0),
                                         pipeline_mode=pl.Buffered(1))
    return pl.pallas_call(
        functools.partial(_mix_out_kernel, hd=hd, tm=tm),
        out_shape=jax.ShapeDtypeStruct((m, d), _F32),
        grid=(m // tm,),
        in_specs=[
            pl.BlockSpec((tm, ka), lambda i: (i, 0)),
            pl.BlockSpec((hd, tm, HEAD_DIM), lambda i: (0, i, 0)),
            resident(ka), resident(hd * HEAD_DIM),
            gate("ga"), gate("gb"),
            resident(d),
            pl.BlockSpec((tm, d), lambda i: (i, 0)),
            pl.BlockSpec((None, 1, d), lambda i: (i // tps, 0, 0)),
        ],
        out_specs=pl.BlockSpec((tm, d), lambda i: (i, 0)),
        scratch_shapes=[pltpu.VMEM((tm, d), _BF16)],
        compiler_params=_cparams(1),
        name="mix_out",
    )(ya, yb, wa, wb, proj, proj, wo, x2, gt)


def _out_res_kernel(a_ref, w_ref, x_ref, gt_ref, o_ref):
    acc = jnp.dot(a_ref[...], w_ref[...], preferred_element_type=_F32)
    o_ref[...] = x_ref[...] + (1.0 + gt_ref[...]) * acc


def _out_res(a, w, layer, x2, gt, seq, tm, tn):
    m, k = a.shape
    d = w.shape[2]
    tps = seq // tm
    return pl.pallas_call(
        _out_res_kernel,
        out_shape=jax.ShapeDtypeStruct((m, d), _F32),
        grid=(m // tm, d // tn),
        in_specs=[
            pl.BlockSpec((tm, k), lambda i, j: (i, 0)),
            pl.BlockSpec((None, k, tn), lambda i, j: (layer, 0, j)),
            pl.BlockSpec((tm, tn), lambda i, j: (i, j)),
            pl.BlockSpec((None, 1, tn), lambda i, j: (i // tps, 0, j)),
        ],
        out_specs=pl.BlockSpec((tm, tn), lambda i, j: (i, j)),
        compiler_params=_cparams(2),
        name="out_res",
    )(a, w, x2, gt)


def _shift_rows(u, k, prev):
    rolled = pltpu.roll(u, k, axis=0)
    first = lax.broadcasted_iota(jnp.int32, prev.shape, 0) < k
    head = jnp.where(first, pltpu.roll(prev, k, axis=0), rolled[:SUBLANES])
    return jnp.concatenate([head, rolled[SUBLANES:]], axis=0)


def _ffn_up_kernel(x_ref, g_ref, sc_ref, sh_ref, wa_ref, wb_ref, cwa_ref, cwb_ref, cba_ref,
                   cbb_ref, o_ref, h_ref, halo_ref, carry_ref, *, tm, tn, tps):
    i = pl.program_id(0)
    j = pl.program_id(1)
    rc = min(ROW_CHUNK, tm)

    @pl.when(j == 0)
    def _():
        h = _norm_modulate(x_ref[...], g_ref[...], sc_ref[...], sh_ref[...])
        h_ref[...] = h.astype(_BF16)

    @pl.when(i % tps == 0)
    def _():
        halo_ref[...] = jnp.zeros_like(halo_ref)

    @pl.when(i % tps != 0)
    def _():
        halo_ref[...] = carry_ref[j]

    def conv(u, prev, cw_ref, cb_ref):
        cw = cw_ref[...]
        u1 = _shift_rows(u, 1, prev)
        u2 = _shift_rows(u, 2, prev)
        return cb_ref[...] + (cw[0:1, :] * u2 + cw[1:2, :] * u1 + cw[2:3, :] * u)

    prev_a = halo_ref[:, 0:tn]
    prev_b = halo_ref[:, tn:2 * tn]
    for r in range(tm // rc):
        lo = r * rc
        hr = h_ref[lo:lo + rc, :]
        ua = jnp.dot(hr, wa_ref[...], preferred_element_type=_F32)
        ub = jnp.dot(hr, wb_ref[...], preferred_element_type=_F32)
        a = conv(ua, prev_a, cwa_ref, cba_ref)
        b = conv(ub, prev_b, cwb_ref, cbb_ref)
        o_ref[lo:lo + rc, :] = (a * jax.nn.sigmoid(a) * b).astype(_BF16)
        prev_a = ua[rc - SUBLANES:, :]
        prev_b = ub[rc - SUBLANES:, :]

    carry_ref[j] = jnp.concatenate([prev_a, prev_b], axis=1)


def _ffn_up(x2, g, sc, sh, w, cw, cb, layer, seq, tm, tn):
    m, d = x2.shape
    nblk = w.shape[2] // (2 * tn)
    tps = seq // tm
    mod_spec = pl.BlockSpec((None, 1, d), lambda i, j: (i // tps, 0, 0))
    halves = lambda rows: (pl.BlockSpec((None, rows, tn), lambda i, j: (layer, 0, j)),
                           pl.BlockSpec((None, rows, tn), lambda i, j: (layer, 0, nblk + j)))
    return pl.pallas_call(
        functools.partial(_ffn_up_kernel, tm=tm, tn=tn, tps=tps),
        out_shape=jax.ShapeDtypeStruct((m, nblk * tn), _BF16),
        grid=(m // tm, nblk),
        in_specs=[
            pl.BlockSpec((tm, d), lambda i, j: (i, 0)),
            pl.BlockSpec((None, 1, d), lambda i, j: (layer, 0, 0)),
            mod_spec, mod_spec,
            *halves(d), *halves(CONV_WIDTH), *halves(1),
        ],
        out_specs=pl.BlockSpec((tm, tn), lambda i, j: (i, j)),
        scratch_shapes=[pltpu.VMEM((tm, d), _BF16),
                        pltpu.VMEM((SUBLANES, 2 * tn), _F32),
                        pltpu.VMEM((nblk, SUBLANES, 2 * tn), _F32)],
        compiler_params=_cparams(2),
        name="ffn_up",
    )(x2, g, sc, sh, w, w, cw, cw, cb, cb)


def _final_norm_kernel(x_ref, g_ref, o_ref):
    x = x_ref[...]
    ms = jnp.mean(x * x, axis=-1, keepdims=True)
    o_ref[...] = (x * lax.rsqrt(ms + NORM_EPS)) * g_ref[...]


def _final_norm(x2, g, tm):
    m, d = x2.shape
    return pl.pallas_call(
        _final_norm_kernel,
        out_shape=jax.ShapeDtypeStruct((m, d), _F32),
        grid=(m // tm,),
        in_specs=[pl.BlockSpec((tm, d), lambda i: (i, 0)),
                  pl.BlockSpec((1, d), lambda i: (0, 0))],
        out_specs=pl.BlockSpec((tm, d), lambda i: (i, 0)),
        compiler_params=_cparams(1),
        name="final_norm",
    )(x2, g)


def _arrange_w_in(w_in, d, hs, hd, hi):
    sw, dw, iw = hs * HEAD_DIM, hd * HEAD_DIM, hi * IDX_DIM
    sizes = (sw, sw, sw, dw, dw, dw, iw, IDX_DIM, hi, d, d)
    offs = [0]
    for s in sizes:
        offs.append(offs[-1] + s)
    qa, ka, va, qb, kb, vb, qi, ki, wi, ga, gb = (
        w_in[..., offs[n]:offs[n + 1]] for n in range(len(sizes)))
    z = lambda n: jnp.zeros(w_in.shape[:-1] + (n,), w_in.dtype)
    misc = jnp.concatenate([ki, z(LANES - IDX_DIM), z(LANES - IDX_DIM), ki,
                            wi, z(LANES - hi), z(LANES)], axis=-1)
    return jnp.concatenate([ga, gb, qa, ka, va, qb, kb, vb, qi, misc], axis=-1).astype(_BF16)


def _rope_tables(positions):
    pos = positions.reshape(-1).astype(_F32)

    def cos_sin(dim):
        inv_freq = 1.0 / (ROPE_THETA ** (jnp.arange(0, dim, 2, dtype=_F32) / dim))
        ang = pos[:, None] * inv_freq
        return jnp.cos(ang), jnp.sin(ang)

    c, s = cos_sin(HEAD_DIM)
    c128 = jnp.concatenate([c, c], axis=-1)
    s128 = jnp.concatenate([-s, s], axis=-1)
    c, s = cos_sin(IDX_DIM)
    zero = jnp.zeros_like(s)
    c64 = jnp.concatenate([c, c, c, c], axis=-1)
    s64lo = jnp.concatenate([-s, zero, -s, zero], axis=-1)
    s64hi = jnp.concatenate([zero, s, zero, s], axis=-1)
    return c128, s128, c64, s64lo, s64hi


def _pick(n, prefs):
    for p in prefs:
        if n % p == 0:
            return p
    return n


def kernel(x, c, positions, w_in, w_a, w_b, w_o, w_ada, b_ada, g_mix, g_ffn, w_up, conv_w,
           conv_b, w_down, g_final):
    batch, seq, d = x.shape
    depth = w_in.shape[0]
    hs = w_a.shape[1] // HEAD_DIM
    hd = w_b.shape[1] // HEAD_DIM
    rest = w_in.shape[2] - 3 * hs * HEAD_DIM - 3 * hd * HEAD_DIM - 2 * d - IDX_DIM
    hi = rest // (IDX_DIM + 1)
    dff = w_down.shape[1]
    m = batch * seq
    lay = _proj_layout(d, hs, hd, hi)

    tm = _pick(seq, (1024, 512, 256, 128))
    tm_mix = _pick(seq, (512, 256, 128))
    t_sb = _pick(seq, (256, 128))
    nh_sb = 2
    tq_dsa = _pick(seq, (256, 128))
    tn_ffn = _pick(dff, (512, 256, 128))
    tn_out = _pick(d, (512, 256, 128))

    w_in_r = _arrange_w_in(w_in, d, hs, hd, hi)
    w_a16, w_b16, w_o16, w_up16, w_down16 = (t.astype(_BF16)
                                             for t in (w_a, w_b, w_o, w_up, w_down))
    tabs = _rope_tables(positions)
    mod = _modulation(c, w_ada, b_ada).reshape(depth, batch, N_MOD, 1, d)
    g_mix3, g_ffn3 = g_mix.reshape(depth, 1, d), g_ffn.reshape(depth, 1, d)
    conv_b3 = conv_b.reshape(depth, 1, -1)

    x2 = x.reshape(m, d)
    for l in range(depth):
        sh1, sc1, gt1, sh2, sc2, gt2 = (mod[l, :, n] for n in range(N_MOD))
        proj, wi = _in_proj(x2, g_mix3, sc1, sh1, w_in_r, l, tabs, lay, hi, seq, tm)
        ya = _sb_attention(proj, lay, batch, seq, hs, t_sb, nh_sb)
        yb = _dsa_attention(proj, wi, lay, batch, seq, hi, hd, tq_dsa)
        x2 = _mix_out(ya, yb, w_a16, w_b16, w_o16, l, proj, x2, gt1, lay, seq, tm_mix)
        g = _ffn_up(x2, g_ffn3, sc2, sh2, w_up16, conv_w, conv_b3, l, seq, tm, tn_ffn)
        x2 = _out_res(g, w_down16, l, x2, gt2, seq, tm, tn_out)
    out = _final_norm(x2, g_final.reshape(1, d), tm)
    return out.reshape(batch, seq, d)
```

```python
import functools
import math

import jax
import jax.numpy as jnp
from jax import lax
from jax.experimental import pallas as pl
from jax.experimental.pallas import tpu as pltpu

HEAD_DIM = 128
IDX_DIM = 64
TOPK_MAX = 256
CONV_WIDTH = 3
ROPE_THETA = 10000.0
NORM_EPS = 1e-6
N_MOD = 6
LANES = 128
SUBLANES = 8
CHUNKS_PER_BLOCK = 4
PROJ_TN = LANES * CHUNKS_PER_BLOCK
VMEM_LIMIT = 56 * 1024 * 1024
INT_MIN = -2 ** 31
I16_MIN, I16_MAX = -2 ** 15, 2 ** 15 - 1
PACK16 = 16
NEG_BIG = -1e30
ROW_CHUNK = 256
LOG2E = math.log2(math.e)

_F32 = jnp.float32
_BF16 = jnp.bfloat16
_NT = (((1,), (1,)), ((), ()))


def _cparams(n_axes):
    return pltpu.CompilerParams(dimension_semantics=("arbitrary",) * n_axes,
                                vmem_limit_bytes=VMEM_LIMIT)


def _mod_kernel(c_ref, w_ref, b_ref, o_ref):
    c = c_ref[...]
    c_act = (c * jax.nn.sigmoid(c)).astype(_BF16)
    acc = jnp.dot(c_act, w_ref[...].astype(_BF16), preferred_element_type=_F32)
    o_ref[...] = acc + b_ref[...]


def _modulation(c, w_ada, b_ada):
    depth, d, n = w_ada.shape
    b = c.shape[0]
    tn = 1024 if n % 1024 == 0 else n
    return pl.pallas_call(
        _mod_kernel,
        out_shape=jax.ShapeDtypeStruct((depth, b, n), _F32),
        grid=(depth, n // tn),
        in_specs=[
            pl.BlockSpec((b, d), lambda l, j: (0, 0)),
            pl.BlockSpec((None, d, tn), lambda l, j: (l, 0, j)),
            pl.BlockSpec((None, 1, tn), lambda l, j: (l, 0, j)),
        ],
        out_specs=pl.BlockSpec((None, b, tn), lambda l, j: (l, 0, j)),
        compiler_params=_cparams(2),
        name="adaln_mod",
    )(c, w_ada, b_ada.reshape(depth, 1, n))


def _norm_modulate(x, g, sc, sh):
    ms = jnp.mean(x * x, axis=-1, keepdims=True)
    r = x * lax.rsqrt(ms + NORM_EPS)
    return (r * g) * (1.0 + sc) + sh


def _proj_layout(d, hs, hd, hi):
    sizes = [("ga", d // LANES), ("gb", d // LANES),
             ("qa", hs), ("ka", hs), ("va", hs),
             ("qb", hd), ("kb", hd), ("vb", hd),
             ("qi", hi * IDX_DIM // LANES)]
    spb = 2 if all(n % (2 * CHUNKS_PER_BLOCK) == 0 for _, n in sizes) else 1
    sizes.append(("misc", spb * CHUNKS_PER_BLOCK))
    off, lay = 0, {"spb": spb}
    for name, n in sizes:
        assert n % (spb * CHUNKS_PER_BLOCK) == 0, (name, n)
        lay[name] = (off, n)
        off += n
    lay["total"] = off
    return lay


def _rope128(t, cos, sin):
    return t * cos + pltpu.roll(t, HEAD_DIM // 2, axis=1) * sin


def _rope64(t, cos, sin_lo, sin_hi):
    half = IDX_DIM // 2
    return (t * cos + pltpu.roll(t, LANES - half, axis=1) * sin_lo
            + pltpu.roll(t, half, axis=1) * sin_hi)


def _in_proj_kernel(x_ref, g_ref, sc_ref, sh_ref, w_ref, c128_ref, s128_ref,
                    c64_ref, s64lo_ref, s64hi_ref, o_ref, wi_ref, h_ref, *, lay, idx_scale, tm):
    j = pl.program_id(1)

    @pl.when(j == 0)
    def _():
        h = _norm_modulate(x_ref[...], g_ref[...], sc_ref[...], sh_ref[...])
        h_ref[...] = h.astype(_BF16)

    spb = lay["spb"]
    step_chunks = spb * CHUNKS_PER_BLOCK
    blk = lambda name: (lay[name][0] // step_chunks, (lay[name][0] + lay[name][1]) // step_chunks)
    qa_scale = (HEAD_DIM ** -0.5) * LOG2E
    qb_scale = HEAD_DIM ** -0.5
    rc = min(ROW_CHUNK, tm)

    def run(epilogue, nsub=spb):
        for sub in range(nsub):
            for r in range(tm // rc):
                rows = slice(r * rc, (r + 1) * rc)
                acc = jnp.dot(h_ref[rows, :], w_ref[:, sub * PROJ_TN:(sub + 1) * PROJ_TN],
                              preferred_element_type=_F32)
                epilogue(acc, rows, sub * CHUNKS_PER_BLOCK)

    def per_chunk(fn):
        def epilogue(acc, rows, c0):
            for c in range(CHUNKS_PER_BLOCK):
                o_ref[c0 + c, rows, :] = fn(acc[:, c * LANES:(c + 1) * LANES], rows).astype(_BF16)
        return epilogue

    def in_group(*names):
        cond = None
        for nm in names:
            lo, hi = blk(nm)
            t = jnp.logical_and(j >= lo, j < hi)
            cond = t if cond is None else jnp.logical_or(cond, t)
        return cond

    rope128 = lambda t, rows: _rope128(t, c128_ref[rows, :], s128_ref[rows, :])
    rope64 = lambda t, rows: _rope64(t, c64_ref[rows, :], s64lo_ref[rows, :], s64hi_ref[rows, :])

    @pl.when(in_group("ga", "gb"))
    def _():
        run(per_chunk(lambda t, rows: jax.nn.sigmoid(t)))

    @pl.when(in_group("qa"))
    def _():
        run(per_chunk(lambda t, rows: t * qa_scale))

    @pl.when(in_group("ka", "va", "vb"))
    def _():
        run(per_chunk(lambda t, rows: t))

    @pl.when(in_group("qb"))
    def _():
        run(per_chunk(lambda t, rows: rope128(t, rows) * qb_scale))

    @pl.when(in_group("kb"))
    def _():
        run(per_chunk(rope128))

    @pl.when(in_group("qi"))
    def _():
        run(per_chunk(rope64))

    @pl.when(in_group("misc"))
    def _():
        def epilogue(acc, rows, c0):
            for c in range(2):
                o_ref[c, rows, :] = rope64(acc[:, c * LANES:(c + 1) * LANES], rows).astype(_BF16)
            o_ref[2, rows, :] = acc[:, 2 * LANES:3 * LANES].astype(_BF16)
            wi_ref[rows, :] = acc[:, 2 * LANES:3 * LANES] * idx_scale
        run(epilogue, nsub=1)
        o_ref[3:step_chunks] = jnp.zeros((step_chunks - 3, tm, LANES), _BF16)


def _in_proj(x2, g, sc, sh, w, layer, tabs, lay, hi, seq, tm):
    m, d = x2.shape
    step_chunks = lay["spb"] * CHUNKS_PER_BLOCK
    nblk = lay["total"] // step_chunks
    tps = seq // tm
    row_spec = pl.BlockSpec((tm, LANES), lambda i, j: (i, 0))
    mod_spec = pl.BlockSpec((None, 1, d), lambda i, j: (i // tps, 0, 0))
    kern = functools.partial(_in_proj_kernel, lay=lay, tm=tm,
                             idx_scale=(IDX_DIM ** -0.5) * (hi ** -0.5))
    return pl.pallas_call(
        kern,
        out_shape=(jax.ShapeDtypeStruct((lay["total"], m, LANES), _BF16),
                   jax.ShapeDtypeStruct((m, LANES), _F32)),
        grid=(m // tm, nblk),
        in_specs=[
            pl.BlockSpec((tm, d), lambda i, j: (i, 0)),
            pl.BlockSpec((None, 1, d), lambda i, j: (layer, 0, 0)),
            mod_spec, mod_spec,
            pl.BlockSpec((None, d, step_chunks * LANES), lambda i, j: (layer, 0, j)),
            row_spec, row_spec, row_spec, row_spec, row_spec,
        ],
        out_specs=(pl.BlockSpec((step_chunks, tm, LANES), lambda i, j: (j, i, 0)),
                   pl.BlockSpec((tm, LANES), lambda i, j: (i, 0))),
        scratch_shapes=[pltpu.VMEM((tm, d), _BF16)],
        compiler_params=_cparams(2),
        name="in_proj",
    )(x2, g, sc, sh, w, *tabs)


def _add_lane_replicated(x, rep):
    n = x.shape[1] // LANES
    return jnp.concatenate([x[:, c * LANES:(c + 1) * LANES] + rep for c in range(n)], axis=1)


def _sb_kernel(q_ref, k_ref, v_ref, tri_ref, o_ref, *, t, nh, nq):
    qi = pl.program_id(2)

    def variant(n):
        past = (lax.broadcasted_iota(jnp.int32, (t, t), 1)
                < lax.broadcasted_iota(jnp.int32, (t, t), 0))

        def front(hh, kb):
            k = k_ref[hh, kb * t:(kb + 1) * t, :]
            z = lax.dot_general(q_ref[hh], k, _NT, preferred_element_type=_F32)
            sp = jnp.maximum(z, 0.0) + jnp.log(1.0 + jnp.exp2(-jnp.abs(z))) * LOG2E
            log1m = -sp
            if kb == n:
                log1m = jnp.where(past, log1m, 0.0)
            l1m = log1m.astype(_BF16)
            cum = jnp.dot(l1m, tri_ref[...], preferred_element_type=_F32)
            total = cum[:, 0:1] + l1m[:, 0:1].astype(_F32)
            return z - sp, cum, total

        def back(hh, kb, zs, cum, carry):
            w = jnp.exp2(zs + _add_lane_replicated(cum, carry))
            if kb == n:
                w = jnp.where(past, w, 0.0)
            v = v_ref[hh, kb * t:(kb + 1) * t, :]
            return jnp.dot(w.astype(_BF16), v, preferred_element_type=_F32)

        items = [(hh, kb) for hh in range(nh) for kb in range(n, -1, -1)]
        ahead = front(*items[0])
        for idx, (hh, kb) in enumerate(items):
            zs, cum, total = ahead
            if idx + 1 < len(items):
                ahead = front(*items[idx + 1])
            if kb == n:
                acc = jnp.zeros((t, HEAD_DIM), _F32)
                carry = jnp.zeros((t, LANES), _F32)
            acc = acc + back(hh, kb, zs, cum, carry)
            carry = carry + jnp.broadcast_to(total, (t, LANES))
            if kb == 0:
                o_ref[:, hh * HEAD_DIM:(hh + 1) * HEAD_DIM] = acc.astype(_BF16)

    for n in range(nq):
        pl.when(qi == n)(functools.partial(variant, n))


def _sb_attention(proj, lay, batch, seq, hs, t, nh):
    m = batch * seq
    nq = seq // t
    qa, ka, va = (lay[n][0] // nh for n in ("qa", "ka", "va"))
    assert hs % nh == 0 and all(lay[n][0] % nh == 0 for n in ("qa", "ka", "va"))
    r = lax.broadcasted_iota(jnp.int32, (t, t), 0)
    c = lax.broadcasted_iota(jnp.int32, (t, t), 1)
    tri = jnp.where(r > c, 1.0, 0.0).astype(_BF16)
    return pl.pallas_call(
        functools.partial(_sb_kernel, t=t, nh=nh, nq=nq),
        out_shape=jax.ShapeDtypeStruct((m, hs * HEAD_DIM), _BF16),
        grid=(batch, hs // nh, nq),
        in_specs=[
            pl.BlockSpec((nh, t, HEAD_DIM), lambda b, h, i: (qa + h, b * nq + i, 0)),
            pl.BlockSpec((nh, seq, HEAD_DIM), lambda b, h, i: (ka + h, b, 0)),
            pl.BlockSpec((nh, seq, HEAD_DIM), lambda b, h, i: (va + h, b, 0)),
            pl.BlockSpec((t, t), lambda b, h, i: (0, 0)),
        ],
        out_specs=pl.BlockSpec((t, nh * HEAD_DIM), lambda b, h, i: (b * nq + i, h)),
        compiler_params=_cparams(3),
        name="sb_attn",
    )(proj, proj, proj, tri)


def _sortable(score):
    u = lax.bitcast_convert_type(score, jnp.int32)
    return u ^ (lax.shift_right_arithmetic(u, 31) & 0x7FFFFFFF)


def _dsa_kernel(qi_ref, ke_ref, wi_ref, qb_ref, kb_ref, vb_ref, lowtri_ref, o_ref,
                wt_ref, vt_ref, score_ref, key_ref, k16_ref, bias_ref, *, tq, seq, hi, hd, topk,
                nbuckets):
    i = pl.program_id(1)
    npairs = hi // 2
    tchunk = 256 if seq % 256 == 0 else LANES

    @pl.when(i == 0)
    def _():
        def xpose(h, c):
            for r in range(seq // tchunk):
                blk = vb_ref[h, r * tchunk:(r + 1) * tchunk, :].astype(_F32)
                vt_ref[h, :, r * tchunk:(r + 1) * tchunk] = blk.T.astype(_BF16)
            return c
        lax.fori_loop(0, hd, xpose, 0)

    wt_ref[...] = wi_ref[...].T

    def run(width):
        key_pos = lax.broadcasted_iota(jnp.int32, (width, tq), 0)
        q_pos = i * tq + lax.broadcasted_iota(jnp.int32, (width, tq), 1)
        visible = key_pos <= q_pos

        def pair(p):
            qp = qi_ref[p]
            d_even = lax.dot_general(ke_ref[0, :width, :], qp, _NT, preferred_element_type=_F32)
            d_odd = lax.dot_general(ke_ref[1, :width, :], qp, _NT, preferred_element_type=_F32)
            return (wt_ref[2 * p:2 * p + 1, :] * jnp.maximum(d_even, 0.0)
                    + wt_ref[2 * p + 1:2 * p + 2, :] * jnp.maximum(d_odd, 0.0))

        score_ref[:width, :] = pair(0)
        for p in range(1, npairs):
            score_ref[:width, :] += pair(p)
        key_ref[:width, :] = jnp.where(visible, _sortable(score_ref[:width, :]), INT_MIN)

        def count_ge16(cand):
            hit = jnp.where(k16_ref[:width, :] >= cand.astype(jnp.int16), jnp.int16(1), jnp.int16(0))
            parts = [hit[r * PACK16:(r + 1) * PACK16, :] for r in range(width // PACK16)]
            while len(parts) > 1:
                odd = parts[-1:] if len(parts) % 2 else []
                parts = [a + b for a, b in zip(parts[0::2], parts[1::2])] + odd
            return jnp.sum(parts[0].astype(jnp.int32), axis=0, keepdims=True)

        def search16(need):
            prefix = jnp.where(count_ge16(jnp.zeros((1, tq), jnp.int32)) >= need, 0, I16_MIN)

            def step(b, prefix):
                cand = prefix + lax.shift_left(jnp.int32(1), 14 - b)
                return jnp.where(count_ge16(cand) >= need, cand, prefix)

            return lax.fori_loop(0, 15, step, prefix.astype(jnp.int32))

        key_hi = lax.shift_right_arithmetic(key_ref[:width, :], 16)
        k16_ref[:width, :] = key_hi.astype(jnp.int16)
        t_hi = search16(jnp.full((1, tq), topk, jnp.int32))
        n_above = jnp.where(t_hi < I16_MAX, count_ge16(jnp.minimum(t_hi + 1, I16_MAX)), 0)
        key = key_ref[:width, :]
        key_lo = (key & 0xFFFF) + I16_MIN
        tied = lax.shift_right_arithmetic(key, 16) == t_hi
        k16_ref[:width, :] = jnp.where(tied, key_lo, I16_MIN).astype(jnp.int16)
        t_lo = search16(topk - n_above)
        thr = t_hi * 65536 + (t_lo - I16_MIN)
        thr = jnp.maximum(thr, INT_MIN + 1)
        selected = key_ref[:width, :] >= thr
        bias_ref[:width, :] = jnp.where(selected, 0.0, NEG_BIG)
        n_selected = jnp.sum(jnp.where(selected, 1.0, 0.0), axis=0, keepdims=True)

        @pl.when(jnp.max(n_selected) > topk)
        def _():
            key = key_ref[:width, :]
            above = key > thr
            need = topk - jnp.sum(jnp.where(above, 1.0, 0.0), axis=0, keepdims=True)
            before = jnp.zeros((1, tq), _F32)
            for r in range(width // tq):
                rows = slice(r * tq, (r + 1) * tq)
                tie = key[rows, :] == thr
                within = jnp.dot(lowtri_ref[...], jnp.where(tie, 1.0, 0.0).astype(_BF16),
                                 preferred_element_type=_F32)
                keep = jnp.logical_and(tie, within + before <= need)
                bias_ref[rows, :] = jnp.where(jnp.logical_or(above[rows, :], keep), 0.0, NEG_BIG)
                before = before + within[tq - 1:tq, :]

        def logits(h):
            s = lax.dot_general(kb_ref[h, :width, :], qb_ref[h], _NT, preferred_element_type=_F32)
            return s + bias_ref[:width, :]

        def finish(h, s):
            mx = jnp.max(s, axis=0, keepdims=True)
            p = jnp.exp(s - mx)
            den = jnp.sum(p, axis=0, keepdims=True)
            ot = jnp.dot(vt_ref[h, :, :width], p.astype(_BF16), preferred_element_type=_F32)
            o_ref[h] = (ot / den).T.astype(_BF16)

        s_next = logits(0)
        for h in range(hd):
            s_cur = s_next
            if h + 1 < hd:
                s_next = logits(h + 1)
            finish(h, s_cur)

    nq = seq // tq
    per = nq // nbuckets
    bucket = i // per
    for bi in range(nbuckets):
        pl.when(bucket == bi)(functools.partial(run, (bi + 1) * per * tq))


def _dsa_attention(proj, wi, lay, batch, seq, hi, hd, tq):
    m = batch * seq
    nq = seq // tq
    topk = min(TOPK_MAX, seq // 4)
    nbuckets = 4 if nq % 4 == 0 else 1
    npairs = hi // 2
    qi_b = lay["qi"][0] // npairs
    ke_b = lay["misc"][0] // 2
    qb_b, kb_b, vb_b = (lay[n][0] // hd for n in ("qb", "kb", "vb"))
    assert lay["qi"][0] % npairs == 0 and lay["misc"][0] % 2 == 0
    assert all(lay[n][0] % hd == 0 for n in ("qb", "kb", "vb"))
    kern = functools.partial(_dsa_kernel, tq=tq, seq=seq, hi=hi, hd=hd, topk=topk,
                             nbuckets=nbuckets)
    r = lax.broadcasted_iota(jnp.int32, (tq, tq), 0)
    c = lax.broadcasted_iota(jnp.int32, (tq, tq), 1)
    lowtri = jnp.where(c <= r, 1.0, 0.0).astype(_BF16)
    return pl.pallas_call(
        kern,
        out_shape=jax.ShapeDtypeStruct((hd, m, HEAD_DIM), _BF16),
        grid=(batch, nq),
        in_specs=[
            pl.BlockSpec((npairs, tq, LANES), lambda b, i: (qi_b, b * nq + i, 0)),
            pl.BlockSpec((2, seq, LANES), lambda b, i: (ke_b, b, 0)),
            pl.BlockSpec((tq, LANES), lambda b, i: (b * nq + i, 0)),
            pl.BlockSpec((hd, tq, HEAD_DIM), lambda b, i: (qb_b, b * nq + i, 0)),
            pl.BlockSpec((hd, seq, HEAD_DIM), lambda b, i: (kb_b, b, 0)),
            pl.BlockSpec((hd, seq, HEAD_DIM), lambda b, i: (vb_b, b, 0)),
            pl.BlockSpec((tq, tq), lambda b, i: (0, 0)),
        ],
        out_specs=pl.BlockSpec((hd, tq, HEAD_DIM), lambda b, i: (0, b * nq + i, 0)),
        scratch_shapes=[pltpu.VMEM((LANES, tq), _F32),
                        pltpu.VMEM((hd, HEAD_DIM, seq), _BF16),
                        pltpu.VMEM((seq, tq), _F32),
                        pltpu.VMEM((seq, tq), jnp.int32),
                        pltpu.VMEM((seq, tq), jnp.int16),
                        pltpu.VMEM((seq, tq), _F32)],
        compiler_params=_cparams(2),
        name="dsa_attn",
    )(proj, proj, wi, proj, proj, proj, lowtri)


def _mix_out_kernel(ya_ref, yb_ref, wa_ref, wb_ref, ga_ref, gb_ref, wo_ref, x_ref, gt_ref,
                    o_ref, mg_ref, *, hd, tm):
    rc = min(ROW_CHUNK, tm)
    for r in range(tm // rc):
        rows = slice(r * rc, (r + 1) * rc)
        yb = jnp.concatenate([yb_ref[h, rows, :] for h in range(hd)], axis=1)
        a = jnp.dot(ya_ref[rows, :], wa_ref[...], preferred_element_type=_F32)
        b = jnp.dot(yb, wb_ref[...], preferred_element_type=_F32)
        for c in range(a.shape[1] // LANES):
            sl = slice(c * LANES, (c + 1) * LANES)
            mg_ref[rows, sl] = (ga_ref[c, rows, :].astype(_F32) * a[:, sl]
                                + gb_ref[c, rows, :].astype(_F32) * b[:, sl]).astype(_BF16)
    for r in range(tm // rc):
        rows = slice(r * rc, (r + 1) * rc)
        acc = jnp.dot(mg_ref[rows, :], wo_ref[...], preferred_element_type=_F32)
        o_ref[rows, :] = x_ref[rows, :] + (1.0 + gt_ref[...]) * acc


def _mix_out(ya, yb, wa, wb, wo, layer, proj, x2, gt, lay, seq, tm):
    m, ka = ya.shape
    hd = yb.shape[0]
    d = wo.shape[2]
    nchunks = d // LANES
    tps = seq // tm
    assert lay["ga"][0] % nchunks == 0 and lay["gb"][0] % nchunks == 0
    gate = lambda name: pl.BlockSpec((nchunks, tm, LANES),
                                     lambda i: (lay[name][0] // nchunks, i, 0))
    resident = lambda rows: pl.BlockSpec((None, rows, d), lambda i: (layer, 0, 0),
                                         pipeline_mode=pl.Buffered(1))
    return pl.pallas_call(
        functools.partial(_mix_out_kernel, hd=hd, tm=tm),
        out_shape=jax.ShapeDtypeStruct((m, d), _F32),
        grid=(m // tm,),
        in_specs=[
            pl.BlockSpec((tm, ka), lambda i: (i, 0)),
            pl.BlockSpec((hd, tm, HEAD_DIM), lambda i: (0, i, 0)),
            resident(ka), resident(hd * HEAD_DIM),
            gate("ga"), gate("gb"),
            resident(d),
            pl.BlockSpec((tm, d), lambda i: (i, 0)),
            pl.BlockSpec((None, 1, d), lambda i: (i // tps, 0, 0)),
        ],
        out_specs=pl.BlockSpec((tm, d), lambda i: (i, 0)),
        scratch_shapes=[pltpu.VMEM((tm, d), _BF16)],
        compiler_params=_cparams(1),
        name="mix_out",
    )(ya, yb, wa, wb, proj, proj, wo, x2, gt)


def _out_res_kernel(a_ref, w_ref, x_ref, gt_ref, o_ref):
    acc = jnp.dot(a_ref[...], w_ref[...], preferred_element_type=_F32)
    o_ref[...] = x_ref[...] + (1.0 + gt_ref[...]) * acc


def _out_res(a, w, layer, x2, gt, seq, tm, tn):
    m, k = a.shape
    d = w.shape[2]
    tps = seq // tm
    return pl.pallas_call(
        _out_res_kernel,
        out_shape=jax.ShapeDtypeStruct((m, d), _F32),
        grid=(m // tm, d // tn),
        in_specs=[
            pl.BlockSpec((tm, k), lambda i, j: (i, 0)),
            pl.BlockSpec((None, k, tn), lambda i, j: (layer, 0, j)),
            pl.BlockSpec((tm, tn), lambda i, j: (i, j)),
            pl.BlockSpec((None, 1, tn), lambda i, j: (i // tps, 0, j)),
        ],
        out_specs=pl.BlockSpec((tm, tn), lambda i, j: (i, j)),
        compiler_params=_cparams(2),
        name="out_res",
    )(a, w, x2, gt)


def _shift_rows(u, k, prev):
    rolled = pltpu.roll(u, k, axis=0)
    first = lax.broadcasted_iota(jnp.int32, prev.shape, 0) < k
    head = jnp.where(first, pltpu.roll(prev, k, axis=0), rolled[:SUBLANES])
    return jnp.concatenate([head, rolled[SUBLANES:]], axis=0)


def _ffn_up_kernel(x_ref, g_ref, sc_ref, sh_ref, wa_ref, wb_ref, cwa_ref, cwb_ref, cba_ref,
                   cbb_ref, o_ref, h_ref, halo_ref, carry_ref, *, tm, tn, tps):
    i = pl.program_id(0)
    j = pl.program_id(1)
    rc = min(ROW_CHUNK, tm)

    @pl.when(j == 0)
    def _():
        h = _norm_modulate(x_ref[...], g_ref[...], sc_ref[...], sh_ref[...])
        h_ref[...] = h.astype(_BF16)

    @pl.when(i % tps == 0)
    def _():
        halo_ref[...] = jnp.zeros_like(halo_ref)

    @pl.when(i % tps != 0)
    def _():
        halo_ref[...] = carry_ref[j]

    def conv(u, prev, cw_ref, cb_ref):
        cw = cw_ref[...]
        u1 = _shift_rows(u, 1, prev)
        u2 = _shift_rows(u, 2, prev)
        return cb_ref[...] + (cw[0:1, :] * u2 + cw[1:2, :] * u1 + cw[2:3, :] * u)

    prev_a = halo_ref[:, 0:tn]
    prev_b = halo_ref[:, tn:2 * tn]
    for r in range(tm // rc):
        lo = r * rc
        hr = h_ref[lo:lo + rc, :]
        ua = jnp.dot(hr, wa_ref[...], preferred_element_type=_F32)
        ub = jnp.dot(hr, wb_ref[...], preferred_element_type=_F32)
        a = conv(ua, prev_a, cwa_ref, cba_ref)
        b = conv(ub, prev_b, cwb_ref, cbb_ref)
        o_ref[lo:lo + rc, :] = (a * jax.nn.sigmoid(a) * b).astype(_BF16)
        prev_a = ua[rc - SUBLANES:, :]
        prev_b = ub[rc - SUBLANES:, :]

    carry_ref[j] = jnp.concatenate([prev_a, prev_b], axis=1)


def _ffn_up(x2, g, sc, sh, w, cw, cb, layer, seq, tm, tn):
    m, d = x2.shape
    nblk = w.shape[2] // (2 * tn)
    tps = seq // tm
    mod_spec = pl.BlockSpec((None, 1, d), lambda i, j: (i // tps, 0, 0))
    halves = lambda rows: (pl.BlockSpec((None, rows, tn), lambda i, j: (layer, 0, j)),
                           pl.BlockSpec((None, rows, tn), lambda i, j: (layer, 0, nblk + j)))
    return pl.pallas_call(
        functools.partial(_ffn_up_kernel, tm=tm, tn=tn, tps=tps),
        out_shape=jax.ShapeDtypeStruct((m, nblk * tn), _BF16),
        grid=(m // tm, nblk),
        in_specs=[
            pl.BlockSpec((tm, d), lambda i, j: (i, 0)),
            pl.BlockSpec((None, 1, d), lambda i, j: (layer, 0, 0)),
            mod_spec, mod_spec,
            *halves(d), *halves(CONV_WIDTH), *halves(1),
        ],
        out_specs=pl.BlockSpec((tm, tn), lambda i, j: (i, j)),
        scratch_shapes=[pltpu.VMEM((tm, d), _BF16),
                        pltpu.VMEM((SUBLANES, 2 * tn), _F32),
                        pltpu.VMEM((nblk, SUBLANES, 2 * tn), _F32)],
        compiler_params=_cparams(2),
        name="ffn_up",
    )(x2, g, sc, sh, w, w, cw, cw, cb, cb)


def _final_norm_kernel(x_ref, g_ref, o_ref):
    x = x_ref[...]
    ms = jnp.mean(x * x, axis=-1, keepdims=True)
    o_ref[...] = (x * lax.rsqrt(ms + NORM_EPS)) * g_ref[...]


def _final_norm(x2, g, tm):
    m, d = x2.shape
    return pl.pallas_call(
        _final_norm_kernel,
        out_shape=jax.ShapeDtypeStruct((m, d), _F32),
        grid=(m // tm,),
        in_specs=[pl.BlockSpec((tm, d), lambda i: (i, 0)),
                  pl.BlockSpec((1, d), lambda i: (0, 0))],
        out_specs=pl.BlockSpec((tm, d), lambda i: (i, 0)),
        compiler_params=_cparams(1),
        name="final_norm",
    )(x2, g)


def _arrange_w_in(w_in, d, hs, hd, hi, misc_chunks):
    sw, dw, iw = hs * HEAD_DIM, hd * HEAD_DIM, hi * IDX_DIM
    sizes = (sw, sw, sw, dw, dw, dw, iw, IDX_DIM, hi, d, d)
    offs = [0]
    for s in sizes:
        offs.append(offs[-1] + s)
    qa, ka, va, qb, kb, vb, qi, ki, wi, ga, gb = (
        w_in[..., offs[n]:offs[n + 1]] for n in range(len(sizes)))
    z = lambda n: jnp.zeros(w_in.shape[:-1] + (n,), w_in.dtype)
    misc = jnp.concatenate([ki, z(LANES - IDX_DIM), z(LANES - IDX_DIM), ki,
                            wi, z(LANES - hi), z((misc_chunks - 3) * LANES)], axis=-1)
    return jnp.concatenate([ga, gb, qa, ka, va, qb, kb, vb, qi, misc], axis=-1).astype(_BF16)


def _rope_tables(positions):
    pos = positions.reshape(-1).astype(_F32)

    def cos_sin(dim):
        inv_freq = 1.0 / (ROPE_THETA ** (jnp.arange(0, dim, 2, dtype=_F32) / dim))
        ang = pos[:, None] * inv_freq
        return jnp.cos(ang), jnp.sin(ang)

    c, s = cos_sin(HEAD_DIM)
    c128 = jnp.concatenate([c, c], axis=-1)
    s128 = jnp.concatenate([-s, s], axis=-1)
    c, s = cos_sin(IDX_DIM)
    zero = jnp.zeros_like(s)
    c64 = jnp.concatenate([c, c, c, c], axis=-1)
    s64lo = jnp.concatenate([-s, zero, -s, zero], axis=-1)
    s64hi = jnp.concatenate([zero, s, zero, s], axis=-1)
    return c128, s128, c64, s64lo, s64hi


def _pick(n, prefs):
    for p in prefs:
        if n % p == 0:
            return p
    return n


def kernel(x, c, positions, w_in, w_a, w_b, w_o, w_ada, b_ada, g_mix, g_ffn, w_up, conv_w,
           conv_b, w_down, g_final):
    batch, seq, d = x.shape
    depth = w_in.shape[0]
    hs = w_a.shape[1] // HEAD_DIM
    hd = w_b.shape[1] // HEAD_DIM
    rest = w_in.shape[2] - 3 * hs * HEAD_DIM - 3 * hd * HEAD_DIM - 2 * d - IDX_DIM
    hi = rest // (IDX_DIM + 1)
    dff = w_down.shape[1]
    m = batch * seq
    lay = _proj_layout(d, hs, hd, hi)

    tm = _pick(seq, (1024, 512, 256, 128))
    tm_mix = _pick(seq, (512, 256, 128))
    t_sb = _pick(seq, (256, 128))
    nh_sb = 2
    tq_dsa = _pick(seq, (256, 128))
    tn_ffn = _pick(dff, (512, 256, 128))
    tn_out = _pick(d, (512, 256, 128))

    w_in_r = _arrange_w_in(w_in, d, hs, hd, hi, lay["misc"][1])
    w_a16, w_b16, w_o16, w_up16, w_down16 = (t.astype(_BF16)
                                             for t in (w_a, w_b, w_o, w_up, w_down))
    tabs = _rope_tables(positions)
    mod = _modulation(c, w_ada, b_ada).reshape(depth, batch, N_MOD, 1, d)
    g_mix3, g_ffn3 = g_mix.reshape(depth, 1, d), g_ffn.reshape(depth, 1, d)
    conv_b3 = conv_b.reshape(depth, 1, -1)

    x2 = x.reshape(m, d)
    for l in range(depth):
        sh1, sc1, gt1, sh2, sc2, gt2 = (mod[l, :, n] for n in range(N_MOD))
        proj, wi = _in_proj(x2, g_mix3, sc1, sh1, w_in_r, l, tabs, lay, hi, seq, tm)
        ya = _sb_attention(proj, lay, batch, seq, hs, t_sb, nh_sb)
        yb = _dsa_attention(proj, wi, lay, batch, seq, hi, hd, tq_dsa)
        x2 = _mix_out(ya, yb, w_a16, w_b16, w_o16, l, proj, x2, gt1, lay, seq, tm_mix)
        g = _ffn_up(x2, g_ffn3, sc2, sh2, w_up16, conv_w, conv_b3, l, seq, tm, tn_ffn)
        x2 = _out_res(g, w_down16, l, x2, gt2, seq, tm, tn_out)
    out = _final_norm(x2, g_final.reshape(1, d), tm)
    return out.reshape(batch, seq, d)
```

```python
import functools
import math

import jax
import jax.numpy as jnp
from jax import lax
from jax.experimental import pallas as pl
from jax.experimental.pallas import tpu as pltpu

HEAD_DIM = 128
IDX_DIM = 64
TOPK_MAX = 256
CONV_WIDTH = 3
ROPE_THETA = 10000.0
NORM_EPS = 1e-6
N_MOD = 6
LANES = 128
SUBLANES = 8
CHUNKS_PER_BLOCK = 4
PROJ_TN = LANES * CHUNKS_PER_BLOCK
PROJ_BLOCKS_PER_STEP = 1
VMEM_LIMIT = 56 * 1024 * 1024
INT_MIN = -2 ** 31
I16_MIN, I16_MAX = -2 ** 15, 2 ** 15 - 1
PACK16 = 16
NEG_BIG = -1e30
ROW_CHUNK = 256
LOG2E = math.log2(math.e)

_F32 = jnp.float32
_BF16 = jnp.bfloat16
_NT = (((1,), (1,)), ((), ()))


def _cparams(n_axes):
    return pltpu.CompilerParams(dimension_semantics=("arbitrary",) * n_axes,
                                vmem_limit_bytes=VMEM_LIMIT)


def _mod_kernel(c_ref, w_ref, b_ref, o_ref):
    c = c_ref[...]
    c_act = (c * jax.nn.sigmoid(c)).astype(_BF16)
    acc = jnp.dot(c_act, w_ref[...].astype(_BF16), preferred_element_type=_F32)
    o_ref[...] = acc + b_ref[...]


def _modulation(c, w_ada, b_ada):
    depth, d, n = w_ada.shape
    b = c.shape[0]
    tn = 1024 if n % 1024 == 0 else n
    return pl.pallas_call(
        _mod_kernel,
        out_shape=jax.ShapeDtypeStruct((depth, b, n), _F32),
        grid=(depth, n // tn),
        in_specs=[
            pl.BlockSpec((b, d), lambda l, j: (0, 0)),
            pl.BlockSpec((None, d, tn), lambda l, j: (l, 0, j)),
            pl.BlockSpec((None, 1, tn), lambda l, j: (l, 0, j)),
        ],
        out_specs=pl.BlockSpec((None, b, tn), lambda l, j: (l, 0, j)),
        compiler_params=_cparams(2),
        name="adaln_mod",
    )(c, w_ada, b_ada.reshape(depth, 1, n))


def _norm_modulate(x, g, sc, sh):
    ms = jnp.mean(x * x, axis=-1, keepdims=True)
    r = x * lax.rsqrt(ms + NORM_EPS)
    return (r * g) * (1.0 + sc) + sh


def _proj_layout(d, hs, hd, hi):
    sizes = [("ga", d // LANES), ("gb", d // LANES),
             ("qa", hs), ("ka", hs), ("va", hs),
             ("qb", hd), ("kb", hd), ("vb", hd),
             ("qi", hi * IDX_DIM // LANES)]
    spb = PROJ_BLOCKS_PER_STEP
    sizes.append(("misc", spb * CHUNKS_PER_BLOCK))
    off, lay = 0, {"spb": spb}
    for name, n in sizes:
        assert n % (spb * CHUNKS_PER_BLOCK) == 0, (name, n)
        lay[name] = (off, n)
        off += n
    lay["total"] = off
    return lay


def _rope128(t, cos, sin):
    return t * cos + pltpu.roll(t, HEAD_DIM // 2, axis=1) * sin


def _rope64(t, cos, sin_lo, sin_hi):
    half = IDX_DIM // 2
    return (t * cos + pltpu.roll(t, LANES - half, axis=1) * sin_lo
            + pltpu.roll(t, half, axis=1) * sin_hi)


def _in_proj_kernel(x_ref, g_ref, sc_ref, sh_ref, w_ref, c128_ref, s128_ref,
                    c64_ref, s64lo_ref, s64hi_ref, o_ref, wi_ref, h_ref, *, lay, idx_scale, tm):
    j = pl.program_id(1)

    @pl.when(j == 0)
    def _():
        h = _norm_modulate(x_ref[...], g_ref[...], sc_ref[...], sh_ref[...])
        h_ref[...] = h.astype(_BF16)

    spb = lay["spb"]
    step_chunks = spb * CHUNKS_PER_BLOCK
    blk = lambda name: (lay[name][0] // step_chunks, (lay[name][0] + lay[name][1]) // step_chunks)
    qa_scale = (HEAD_DIM ** -0.5) * LOG2E
    qb_scale = HEAD_DIM ** -0.5
    rc = min(ROW_CHUNK, tm)

    def run(epilogue, nsub=spb):
        for sub in range(nsub):
            for r in range(tm // rc):
                rows = slice(r * rc, (r + 1) * rc)
                acc = jnp.dot(h_ref[rows, :], w_ref[:, sub * PROJ_TN:(sub + 1) * PROJ_TN],
                              preferred_element_type=_F32)
                epilogue(acc, rows, sub * CHUNKS_PER_BLOCK)

    def per_chunk(fn):
        def epilogue(acc, rows, c0):
            for c in range(CHUNKS_PER_BLOCK):
                o_ref[c0 + c, rows, :] = fn(acc[:, c * LANES:(c + 1) * LANES], rows).astype(_BF16)
        return epilogue

    def in_group(*names):
        cond = None
        for nm in names:
            lo, hi = blk(nm)
            t = jnp.logical_and(j >= lo, j < hi)
            cond = t if cond is None else jnp.logical_or(cond, t)
        return cond

    rope128 = lambda t, rows: _rope128(t, c128_ref[rows, :], s128_ref[rows, :])
    rope64 = lambda t, rows: _rope64(t, c64_ref[rows, :], s64lo_ref[rows, :], s64hi_ref[rows, :])

    @pl.when(in_group("ga", "gb"))
    def _():
        run(per_chunk(lambda t, rows: jax.nn.sigmoid(t)))

    @pl.when(in_group("qa"))
    def _():
        run(per_chunk(lambda t, rows: t * qa_scale))

    @pl.when(in_group("ka", "va", "vb"))
    def _():
        run(per_chunk(lambda t, rows: t))

    @pl.when(in_group("qb"))
    def _():
        run(per_chunk(lambda t, rows: rope128(t, rows) * qb_scale))

    @pl.when(in_group("kb"))
    def _():
        run(per_chunk(rope128))

    @pl.when(in_group("qi"))
    def _():
        run(per_chunk(rope64))

    @pl.when(in_group("misc"))
    def _():
        def epilogue(acc, rows, c0):
            for c in range(2):
                o_ref[c, rows, :] = rope64(acc[:, c * LANES:(c + 1) * LANES], rows).astype(_BF16)
            o_ref[2, rows, :] = acc[:, 2 * LANES:3 * LANES].astype(_BF16)
            wi_ref[rows, :] = acc[:, 2 * LANES:3 * LANES] * idx_scale
        run(epilogue, nsub=1)
        o_ref[3:step_chunks] = jnp.zeros((step_chunks - 3, tm, LANES), _BF16)


def _in_proj(x2, g, sc, sh, w, layer, tabs, lay, hi, seq, tm):
    m, d = x2.shape
    step_chunks = lay["spb"] * CHUNKS_PER_BLOCK
    nblk = lay["total"] // step_chunks
    tps = seq // tm
    row_spec = pl.BlockSpec((tm, LANES), lambda i, j: (i, 0))
    mod_spec = pl.BlockSpec((None, 1, d), lambda i, j: (i // tps, 0, 0))
    kern = functools.partial(_in_proj_kernel, lay=lay, tm=tm,
                             idx_scale=(IDX_DIM ** -0.5) * (hi ** -0.5))
    return pl.pallas_call(
        kern,
        out_shape=(jax.ShapeDtypeStruct((lay["total"], m, LANES), _BF16),
                   jax.ShapeDtypeStruct((m, LANES), _F32)),
        grid=(m // tm, nblk),
        in_specs=[
            pl.BlockSpec((tm, d), lambda i, j: (i, 0)),
            pl.BlockSpec((None, 1, d), lambda i, j: (layer, 0, 0)),
            mod_spec, mod_spec,
            pl.BlockSpec((None, d, step_chunks * LANES), lambda i, j: (layer, 0, j)),
            row_spec, row_spec, row_spec, row_spec, row_spec,
        ],
        out_specs=(pl.BlockSpec((step_chunks, tm, LANES), lambda i, j: (j, i, 0)),
                   pl.BlockSpec((tm, LANES), lambda i, j: (i, 0))),
        scratch_shapes=[pltpu.VMEM((tm, d), _BF16)],
        compiler_params=_cparams(2),
        name="in_proj",
    )(x2, g, sc, sh, w, *tabs)


def _add_lane_replicated(x, rep):
    n = x.shape[1] // LANES
    return jnp.concatenate([x[:, c * LANES:(c + 1) * LANES] + rep for c in range(n)], axis=1)


def _sb_kernel(q_ref, k_ref, v_ref, tri_ref, o_ref, *, t, nh, nq):
    qi = pl.program_id(2)

    def variant(n):
        past = (lax.broadcasted_iota(jnp.int32, (t, t), 1)
                < lax.broadcasted_iota(jnp.int32, (t, t), 0))

        def front(hh, kb):
            k = k_ref[hh, kb * t:(kb + 1) * t, :]
            z = lax.dot_general(q_ref[hh], k, _NT, preferred_element_type=_F32)
            sp = jnp.maximum(z, 0.0) + jnp.log(1.0 + jnp.exp2(-jnp.abs(z))) * LOG2E
            log1m = -sp
            if kb == n:
                log1m = jnp.where(past, log1m, 0.0)
            l1m = log1m.astype(_BF16)
            cum = jnp.dot(l1m, tri_ref[...], preferred_element_type=_F32)
            total = cum[:, 0:1] + l1m[:, 0:1].astype(_F32)
            return z - sp, cum, total

        def back(hh, kb, zs, cum, carry):
            w = jnp.exp2(zs + _add_lane_replicated(cum, carry))
            if kb == n:
                w = jnp.where(past, w, 0.0)
            v = v_ref[hh, kb * t:(kb + 1) * t, :]
            return jnp.dot(w.astype(_BF16), v, preferred_element_type=_F32)

        items = [(hh, kb) for hh in range(nh) for kb in range(n, -1, -1)]
        ahead = front(*items[0])
        for idx, (hh, kb) in enumerate(items):
            zs, cum, total = ahead
            if idx + 1 < len(items):
                ahead = front(*items[idx + 1])
            if kb == n:
                acc = jnp.zeros((t, HEAD_DIM), _F32)
                carry = jnp.zeros((t, LANES), _F32)
            acc = acc + back(hh, kb, zs, cum, carry)
            carry = carry + jnp.broadcast_to(total, (t, LANES))
            if kb == 0:
                o_ref[:, hh * HEAD_DIM:(hh + 1) * HEAD_DIM] = acc.astype(_BF16)

    for n in range(nq):
        pl.when(qi == n)(functools.partial(variant, n))


def _sb_attention(proj, lay, batch, seq, hs, t, nh):
    m = batch * seq
    nq = seq // t
    qa, ka, va = (lay[n][0] // nh for n in ("qa", "ka", "va"))
    assert hs % nh == 0 and all(lay[n][0] % nh == 0 for n in ("qa", "ka", "va"))
    r = lax.broadcasted_iota(jnp.int32, (t, t), 0)
    c = lax.broadcasted_iota(jnp.int32, (t, t), 1)
    tri = jnp.where(r > c, 1.0, 0.0).astype(_BF16)
    return pl.pallas_call(
        functools.partial(_sb_kernel, t=t, nh=nh, nq=nq),
        out_shape=jax.ShapeDtypeStruct((m, hs * HEAD_DIM), _BF16),
        grid=(batch, hs // nh, nq),
        in_specs=[
            pl.BlockSpec((nh, t, HEAD_DIM), lambda b, h, i: (qa + h, b * nq + i, 0)),
            pl.BlockSpec((nh, seq, HEAD_DIM), lambda b, h, i: (ka + h, b, 0)),
            pl.BlockSpec((nh, seq, HEAD_DIM), lambda b, h, i: (va + h, b, 0)),
            pl.BlockSpec((t, t), lambda b, h, i: (0, 0)),
        ],
        out_specs=pl.BlockSpec((t, nh * HEAD_DIM), lambda b, h, i: (b * nq + i, h)),
        compiler_params=_cparams(3),
        name="sb_attn",
    )(proj, proj, proj, tri)


def _sortable(score):
    u = lax.bitcast_convert_type(score, jnp.int32)
    return u ^ (lax.shift_right_arithmetic(u, 31) & 0x7FFFFFFF)


def _dsa_kernel(qi_ref, ke_ref, wi_ref, qb_ref, kb_ref, vb_ref, lowtri_ref, o_ref,
                wt_ref, vt_ref, score_ref, key_ref, k16_ref, bias_ref, *, tq, seq, hi, hd, topk,
                nbuckets):
    i = pl.program_id(1)
    npairs = hi // 2
    tchunk = 256 if seq % 256 == 0 else LANES

    @pl.when(i == 0)
    def _():
        def xpose(h, c):
            for r in range(seq // tchunk):
                blk = vb_ref[h, r * tchunk:(r + 1) * tchunk, :].astype(_F32)
                vt_ref[h, :, r * tchunk:(r + 1) * tchunk] = blk.T.astype(_BF16)
            return c
        lax.fori_loop(0, hd, xpose, 0)

    wt_ref[...] = wi_ref[...].T

    def run(width):
        key_pos = lax.broadcasted_iota(jnp.int32, (width, tq), 0)
        q_pos = i * tq + lax.broadcasted_iota(jnp.int32, (width, tq), 1)
        visible = key_pos <= q_pos

        def pair(p):
            qp = qi_ref[p]
            d_even = lax.dot_general(ke_ref[0, :width, :], qp, _NT, preferred_element_type=_F32)
            d_odd = lax.dot_general(ke_ref[1, :width, :], qp, _NT, preferred_element_type=_F32)
            return (wt_ref[pl.ds(2 * p, 1), :] * jnp.maximum(d_even, 0.0)
                    + wt_ref[pl.ds(2 * p + 1, 1), :] * jnp.maximum(d_odd, 0.0))

        score_ref[:width, :] = jnp.zeros((width, tq), _F32)
        group = _pick(npairs, (2, 1))

        def pairs(g, c):
            for u in range(group):
                score_ref[:width, :] += pair(g * group + u)
            return c

        lax.fori_loop(0, npairs // group, pairs, 0)
        key_ref[:width, :] = jnp.where(visible, _sortable(score_ref[:width, :]), INT_MIN)

        def count_ge16(cand):
            hit = jnp.where(k16_ref[:width, :] >= cand.astype(jnp.int16), jnp.int16(1), jnp.int16(0))
            parts = [hit[r * PACK16:(r + 1) * PACK16, :] for r in range(width // PACK16)]
            while len(parts) > 1:
                odd = parts[-1:] if len(parts) % 2 else []
                parts = [a + b for a, b in zip(parts[0::2], parts[1::2])] + odd
            return jnp.sum(parts[0].astype(jnp.int32), axis=0, keepdims=True)

        def search16(need):
            prefix = jnp.where(count_ge16(jnp.zeros((1, tq), jnp.int32)) >= need, 0, I16_MIN)

            def step(b, prefix):
                cand = prefix + lax.shift_left(jnp.int32(1), 14 - b)
                return jnp.where(count_ge16(cand) >= need, cand, prefix)

            return lax.fori_loop(0, 15, step, prefix.astype(jnp.int32))

        key_hi = lax.shift_right_arithmetic(key_ref[:width, :], 16)
        k16_ref[:width, :] = key_hi.astype(jnp.int16)
        t_hi = search16(jnp.full((1, tq), topk, jnp.int32))
        n_above = jnp.where(t_hi < I16_MAX, count_ge16(jnp.minimum(t_hi + 1, I16_MAX)), 0)
        key = key_ref[:width, :]
        key_lo = (key & 0xFFFF) + I16_MIN
        tied = lax.shift_right_arithmetic(key, 16) == t_hi
        k16_ref[:width, :] = jnp.where(tied, key_lo, I16_MIN).astype(jnp.int16)
        t_lo = search16(topk - n_above)
        thr = t_hi * 65536 + (t_lo - I16_MIN)
        thr = jnp.maximum(thr, INT_MIN + 1)
        selected = key_ref[:width, :] >= thr
        bias_ref[:width, :] = jnp.where(selected, 0.0, NEG_BIG)
        n_selected = jnp.sum(jnp.where(selected, 1.0, 0.0), axis=0, keepdims=True)

        @pl.when(jnp.max(n_selected) > topk)
        def _():
            key = key_ref[:width, :]
            above = key > thr
            need = topk - jnp.sum(jnp.where(above, 1.0, 0.0), axis=0, keepdims=True)

            def block(r, before):
                rows = pl.ds(pl.multiple_of(r * tq, tq), tq)
                kblk = key_ref[rows, :]
                tie = kblk == thr
                within = jnp.dot(lowtri_ref[...], jnp.where(tie, 1.0, 0.0).astype(_BF16),
                                 preferred_element_type=_F32)
                keep = jnp.logical_and(tie, within + before <= need)
                bias_ref[rows, :] = jnp.where(jnp.logical_or(kblk > thr, keep), 0.0, NEG_BIG)
                return before + within[tq - 1:tq, :]

            lax.fori_loop(0, width // tq, block, jnp.zeros((1, tq), _F32))

        def logits(h):
            s = lax.dot_general(kb_ref[h, :width, :], qb_ref[h], _NT, preferred_element_type=_F32)
            return s + bias_ref[:width, :]

        def finish(h, s):
            mx = jnp.max(s, axis=0, keepdims=True)
            p = jnp.exp(s - mx)
            den = jnp.sum(p, axis=0, keepdims=True)
            ot = jnp.dot(vt_ref[h, :, :width], p.astype(_BF16), preferred_element_type=_F32)
            o_ref[h] = (ot / den).T.astype(_BF16)

        group_h = _pick(hd, (4, 2, 1))

        def heads(g, c):
            h0 = g * group_h
            s_next = logits(h0)
            for u in range(group_h):
                s_cur = s_next
                if u + 1 < group_h:
                    s_next = logits(h0 + u + 1)
                finish(h0 + u, s_cur)
            return c

        lax.fori_loop(0, hd // group_h, heads, 0)

    nq = seq // tq
    per = nq // nbuckets
    bucket = i // per
    for bi in range(nbuckets):
        pl.when(bucket == bi)(functools.partial(run, (bi + 1) * per * tq))


def _dsa_attention(proj, wi, lay, batch, seq, hi, hd, tq):
    m = batch * seq
    nq = seq // tq
    topk = min(TOPK_MAX, seq // 4)
    nbuckets = 4 if nq % 4 == 0 else 1
    npairs = hi // 2
    qi_b = lay["qi"][0] // npairs
    ke_b = lay["misc"][0] // 2
    qb_b, kb_b, vb_b = (lay[n][0] // hd for n in ("qb", "kb", "vb"))
    assert lay["qi"][0] % npairs == 0 and lay["misc"][0] % 2 == 0
    assert all(lay[n][0] % hd == 0 for n in ("qb", "kb", "vb"))
    kern = functools.partial(_dsa_kernel, tq=tq, seq=seq, hi=hi, hd=hd, topk=topk,
                             nbuckets=nbuckets)
    r = lax.broadcasted_iota(jnp.int32, (tq, tq), 0)
    c = lax.broadcasted_iota(jnp.int32, (tq, tq), 1)
    lowtri = jnp.where(c <= r, 1.0, 0.0).astype(_BF16)
    return pl.pallas_call(
        kern,
        out_shape=jax.ShapeDtypeStruct((hd, m, HEAD_DIM), _BF16),
        grid=(batch, nq),
        in_specs=[
            pl.BlockSpec((npairs, tq, LANES), lambda b, i: (qi_b, b * nq + i, 0)),
            pl.BlockSpec((2, seq, LANES), lambda b, i: (ke_b, b, 0)),
            pl.BlockSpec((tq, LANES), lambda b, i: (b * nq + i, 0)),
            pl.BlockSpec((hd, tq, HEAD_DIM), lambda b, i: (qb_b, b * nq + i, 0)),
            pl.BlockSpec((hd, seq, HEAD_DIM), lambda b, i: (kb_b, b, 0)),
            pl.BlockSpec((hd, seq, HEAD_DIM), lambda b, i: (vb_b, b, 0)),
            pl.BlockSpec((tq, tq), lambda b, i: (0, 0)),
        ],
        out_specs=pl.BlockSpec((hd, tq, HEAD_DIM), lambda b, i: (0, b * nq + i, 0)),
        scratch_shapes=[pltpu.VMEM((LANES, tq), _F32),
                        pltpu.VMEM((hd, HEAD_DIM, seq), _BF16),
                        pltpu.VMEM((seq, tq), _F32),
                        pltpu.VMEM((seq, tq), jnp.int32),
                        pltpu.VMEM((seq, tq), jnp.int16),
                        pltpu.VMEM((seq, tq), _F32)],
        compiler_params=_cparams(2),
        name="dsa_attn",
    )(proj, proj, wi, proj, proj, proj, lowtri)


def _mix_out_kernel(ya_ref, yb_ref, wa_ref, wb_ref, ga_ref, gb_ref, wo_ref, x_ref, gt_ref,
                    o_ref, mg_ref, *, hd, tm):
    rc = min(ROW_CHUNK, tm)
    for r in range(tm // rc):
        rows = slice(r * rc, (r + 1) * rc)
        yb = jnp.concatenate([yb_ref[h, rows, :] for h in range(hd)], axis=1)
        a = jnp.dot(ya_ref[rows, :], wa_ref[...], preferred_element_type=_F32)
        b = jnp.dot(yb, wb_ref[...], preferred_element_type=_F32)
        for c in range(a.shape[1] // LANES):
            sl = slice(c * LANES, (c + 1) * LANES)
            mg_ref[rows, sl] = (ga_ref[c, rows, :].astype(_F32) * a[:, sl]
                                + gb_ref[c, rows, :].astype(_F32) * b[:, sl]).astype(_BF16)
    for r in range(tm // rc):
        rows = slice(r * rc, (r + 1) * rc)
        acc = jnp.dot(mg_ref[rows, :], wo_ref[...], preferred_element_type=_F32)
        o_ref[rows, :] = x_ref[rows, :] + (1.0 + gt_ref[...]) * acc


def _mix_out(ya, yb, wa, wb, wo, layer, proj, x2, gt, lay, seq, tm):
    m, ka = ya.shape
    hd = yb.shape[0]
    d = wo.shape[2]
    nchunks = d // LANES
    tps = seq // tm
    assert lay["ga"][0] % nchunks == 0 and lay["gb"][0] % nchunks == 0
    gate = lambda name: pl.BlockSpec((nchunks, tm, LANES),
                                     lambda i: (lay[name][0] // nchunks, i, 0))
    resident = lambda rows: pl.BlockSpec((None, rows, d), lambda i: (layer, 0, 0),
                                         pipeline_mode=pl.Buffered(1))
    return pl.pallas_call(
        functools.partial(_mix_out_kernel, hd=hd, tm=tm),
        out_shape=jax.ShapeDtypeStruct((m, d), _F32),
        grid=(m // tm,),
        in_specs=[
            pl.BlockSpec((tm, ka), lambda i: (i, 0)),
            pl.BlockSpec((hd, tm, HEAD_DIM), lambda i: (0, i, 0)),
            resident(ka), resident(hd * HEAD_DIM),
            gate("ga"), gate("gb"),
            resident(d),
            pl.BlockSpec((tm, d), lambda i: (i, 0)),
            pl.BlockSpec((None, 1, d), lambda i: (i // tps, 0, 0)),
        ],
        out_specs=pl.BlockSpec((tm, d), lambda i: (i, 0)),
        scratch_shapes=[pltpu.VMEM((tm, d), _BF16)],
        compiler_params=_cparams(1),
        name="mix_out",
    )(ya, yb, wa, wb, proj, proj, wo, x2, gt)


def _out_res_kernel(a_ref, w_ref, x_ref, gt_ref, o_ref):
    acc = jnp.dot(a_ref[...], w_ref[...], preferred_element_type=_F32)
    o_ref[...] = x_ref[...] + (1.0 + gt_ref[...]) * acc


def _out_res(a, w, layer, x2, gt, seq, tm, tn):
    m, k = a.shape
    d = w.shape[2]
    tps = seq // tm
    return pl.pallas_call(
        _out_res_kernel,
        out_shape=jax.ShapeDtypeStruct((m, d), _F32),
        grid=(m // tm, d // tn),
        in_specs=[
            pl.BlockSpec((tm, k), lambda i, j: (i, 0)),
            pl.BlockSpec((None, k, tn), lambda i, j: (layer, 0, j)),
            pl.BlockSpec((tm, tn), lambda i, j: (i, j)),
            pl.BlockSpec((None, 1, tn), lambda i, j: (i // tps, 0, j)),
        ],
        out_specs=pl.BlockSpec((tm, tn), lambda i, j: (i, j)),
        compiler_params=_cparams(2),
        name="out_res",
    )(a, w, x2, gt)


def _shift_rows(u, k, prev):
    rolled = pltpu.roll(u, k, axis=0)
    first = lax.broadcasted_iota(jnp.int32, prev.shape, 0) < k
    head = jnp.where(first, pltpu.roll(prev, k, axis=0), rolled[:SUBLANES])
    return jnp.concatenate([head, rolled[SUBLANES:]], axis=0)


def _ffn_up_kernel(x_ref, g_ref, sc_ref, sh_ref, wa_ref, wb_ref, cwa_ref, cwb_ref, cba_ref,
                   cbb_ref, o_ref, h_ref, halo_ref, carry_ref, *, tm, tn, tps):
    i = pl.program_id(0)
    j = pl.program_id(1)
    rc = min(ROW_CHUNK, tm)

    @pl.when(j == 0)
    def _():
        h = _norm_modulate(x_ref[...], g_ref[...], sc_ref[...], sh_ref[...])
        h_ref[...] = h.astype(_BF16)

    @pl.when(i % tps == 0)
    def _():
        halo_ref[...] = jnp.zeros_like(halo_ref)

    @pl.when(i % tps != 0)
    def _():
        halo_ref[...] = carry_ref[j]

    def conv(u, prev, cw_ref, cb_ref):
        cw = cw_ref[...]
        u1 = _shift_rows(u, 1, prev)
        u2 = _shift_rows(u, 2, prev)
        return cb_ref[...] + (cw[0:1, :] * u2 + cw[1:2, :] * u1 + cw[2:3, :] * u)

    prev_a = halo_ref[:, 0:tn]
    prev_b = halo_ref[:, tn:2 * tn]
    for r in range(tm // rc):
        lo = r * rc
        hr = h_ref[lo:lo + rc, :]
        ua = jnp.dot(hr, wa_ref[...], preferred_element_type=_F32)
        ub = jnp.dot(hr, wb_ref[...], preferred_element_type=_F32)
        a = conv(ua, prev_a, cwa_ref, cba_ref)
        b = conv(ub, prev_b, cwb_ref, cbb_ref)
        o_ref[lo:lo + rc, :] = (a * jax.nn.sigmoid(a) * b).astype(_BF16)
        prev_a = ua[rc - SUBLANES:, :]
        prev_b = ub[rc - SUBLANES:, :]

    carry_ref[j] = jnp.concatenate([prev_a, prev_b], axis=1)


def _ffn_up(x2, g, sc, sh, w, cw, cb, layer, seq, tm, tn):
    m, d = x2.shape
    nblk = w.shape[2] // (2 * tn)
    tps = seq // tm
    mod_spec = pl.BlockSpec((None, 1, d), lambda i, j: (i // tps, 0, 0))
    halves = lambda rows: (pl.BlockSpec((None, rows, tn), lambda i, j: (layer, 0, j)),
                           pl.BlockSpec((None, rows, tn), lambda i, j: (layer, 0, nblk + j)))
    return pl.pallas_call(
        functools.partial(_ffn_up_kernel, tm=tm, tn=tn, tps=tps),
        out_shape=jax.ShapeDtypeStruct((m, nblk * tn), _BF16),
        grid=(m // tm, nblk),
        in_specs=[
            pl.BlockSpec((tm, d), lambda i, j: (i, 0)),
            pl.BlockSpec((None, 1, d), lambda i, j: (layer, 0, 0)),
            mod_spec, mod_spec,
            *halves(d), *halves(CONV_WIDTH), *halves(1),
        ],
        out_specs=pl.BlockSpec((tm, tn), lambda i, j: (i, j)),
        scratch_shapes=[pltpu.VMEM((tm, d), _BF16),
                        pltpu.VMEM((SUBLANES, 2 * tn), _F32),
                        pltpu.VMEM((nblk, SUBLANES, 2 * tn), _F32)],
        compiler_params=_cparams(2),
        name="ffn_up",
    )(x2, g, sc, sh, w, w, cw, cw, cb, cb)


def _final_norm_kernel(x_ref, g_ref, o_ref):
    x = x_ref[...]
    ms = jnp.mean(x * x, axis=-1, keepdims=True)
    o_ref[...] = (x * lax.rsqrt(ms + NORM_EPS)) * g_ref[...]


def _final_norm(x2, g, tm):
    m, d = x2.shape
    return pl.pallas_call(
        _final_norm_kernel,
        out_shape=jax.ShapeDtypeStruct((m, d), _F32),
        grid=(m // tm,),
        in_specs=[pl.BlockSpec((tm, d), lambda i: (i, 0)),
                  pl.BlockSpec((1, d), lambda i: (0, 0))],
        out_specs=pl.BlockSpec((tm, d), lambda i: (i, 0)),
        compiler_params=_cparams(1),
        name="final_norm",
    )(x2, g)


def _arrange_w_in(w_in, d, hs, hd, hi, misc_chunks):
    sw, dw, iw = hs * HEAD_DIM, hd * HEAD_DIM, hi * IDX_DIM
    sizes = (sw, sw, sw, dw, dw, dw, iw, IDX_DIM, hi, d, d)
    offs = [0]
    for s in sizes:
        offs.append(offs[-1] + s)
    qa, ka, va, qb, kb, vb, qi, ki, wi, ga, gb = (
        w_in[..., offs[n]:offs[n + 1]] for n in range(len(sizes)))
    z = lambda n: jnp.zeros(w_in.shape[:-1] + (n,), w_in.dtype)
    misc = jnp.concatenate([ki, z(LANES - IDX_DIM), z(LANES - IDX_DIM), ki,
                            wi, z(LANES - hi), z((misc_chunks - 3) * LANES)], axis=-1)
    return jnp.concatenate([ga, gb, qa, ka, va, qb, kb, vb, qi, misc], axis=-1).astype(_BF16)


def _rope_tables(positions):
    pos = positions.reshape(-1).astype(_F32)

    def cos_sin(dim):
        inv_freq = 1.0 / (ROPE_THETA ** (jnp.arange(0, dim, 2, dtype=_F32) / dim))
        ang = pos[:, None] * inv_freq
        return jnp.cos(ang), jnp.sin(ang)

    c, s = cos_sin(HEAD_DIM)
    c128 = jnp.concatenate([c, c], axis=-1)
    s128 = jnp.concatenate([-s, s], axis=-1)
    c, s = cos_sin(IDX_DIM)
    zero = jnp.zeros_like(s)
    c64 = jnp.concatenate([c, c, c, c], axis=-1)
    s64lo = jnp.concatenate([-s, zero, -s, zero], axis=-1)
    s64hi = jnp.concatenate([zero, s, zero, s], axis=-1)
    return c128, s128, c64, s64lo, s64hi


def _pick(n, prefs):
    for p in prefs:
        if n % p == 0:
            return p
    return n


def kernel(x, c, positions, w_in, w_a, w_b, w_o, w_ada, b_ada, g_mix, g_ffn, w_up, conv_w,
           conv_b, w_down, g_final):
    batch, seq, d = x.shape
    depth = w_in.shape[0]
    hs = w_a.shape[1] // HEAD_DIM
    hd = w_b.shape[1] // HEAD_DIM
    rest = w_in.shape[2] - 3 * hs * HEAD_DIM - 3 * hd * HEAD_DIM - 2 * d - IDX_DIM
    hi = rest // (IDX_DIM + 1)
    dff = w_down.shape[1]
    m = batch * seq
    lay = _proj_layout(d, hs, hd, hi)

    tm = _pick(seq, (1024, 512, 256, 128))
    tm_mix = _pick(seq, (512, 256, 128))
    t_sb = _pick(seq, (256, 128))
    nh_sb = 2
    tq_dsa = _pick(seq, (256, 128))
    tn_ffn = _pick(dff, (512, 256, 128))
    tn_out = _pick(d, (512, 256, 128))

    w_in_r = _arrange_w_in(w_in, d, hs, hd, hi, lay["misc"][1])
    w_a16, w_b16, w_o16, w_up16, w_down16 = (t.astype(_BF16)
                                             for t in (w_a, w_b, w_o, w_up, w_down))
    tabs = _rope_tables(positions)
    mod = _modulation(c, w_ada, b_ada).reshape(depth, batch, N_MOD, 1, d)
    g_mix3, g_ffn3 = g_mix.reshape(depth, 1, d), g_ffn.reshape(depth, 1, d)
    conv_b3 = conv_b.reshape(depth, 1, -1)

    x2 = x.reshape(m, d)
    for l in range(depth):
        sh1, sc1, gt1, sh2, sc2, gt2 = (mod[l, :, n] for n in range(N_MOD))
        proj, wi = _in_proj(x2, g_mix3, sc1, sh1, w_in_r, l, tabs, lay, hi, seq, tm)
        ya = _sb_attention(proj, lay, batch, seq, hs, t_sb, nh_sb)
        yb = _dsa_attention(proj, wi, lay, batch, seq, hi, hd, tq_dsa)
        x2 = _mix_out(ya, yb, w_a16, w_b16, w_o16, l, proj, x2, gt1, lay, seq, tm_mix)
        g = _ffn_up(x2, g_ffn3, sc2, sh2, w_up16, conv_w, conv_b3, l, seq, tm, tn_ffn)
        x2 = _out_res(g, w_down16, l, x2, gt2, seq, tm, tn_out)
    out = _final_norm(x2, g_final.reshape(1, d), tm)
    return out.reshape(batch, seq, d)
```

```python
import functools
import math

import jax
import jax.numpy as jnp
from jax import lax
from jax.experimental import pallas as pl
from jax.experimental.pallas import tpu as pltpu

HEAD_DIM = 128
IDX_DIM = 64
TOPK_MAX = 256
CONV_WIDTH = 3
ROPE_THETA = 10000.0
NORM_EPS = 1e-6
N_MOD = 6
LANES = 128
SUBLANES = 8
CHUNKS_PER_BLOCK = 4
PROJ_TN = LANES * CHUNKS_PER_BLOCK
PROJ_BLOCKS_PER_STEP = 1
VMEM_LIMIT = 56 * 1024 * 1024
INT_MIN = -2 ** 31
I16_MIN, I16_MAX = -2 ** 15, 2 ** 15 - 1
PACK16 = 16
NEG_BIG = -1e30
ROW_CHUNK = 256
LOG2E = math.log2(math.e)

_F32 = jnp.float32
_BF16 = jnp.bfloat16
_NT = (((1,), (1,)), ((), ()))


def _cparams(n_axes):
    return pltpu.CompilerParams(dimension_semantics=("arbitrary",) * n_axes,
                                vmem_limit_bytes=VMEM_LIMIT)


def _mod_kernel(c_ref, w_ref, b_ref, o_ref):
    c = c_ref[...]
    c_act = (c * jax.nn.sigmoid(c)).astype(_BF16)
    acc = jnp.dot(c_act, w_ref[...].astype(_BF16), preferred_element_type=_F32)
    o_ref[...] = acc + b_ref[...]


def _modulation(c, w_ada, b_ada):
    depth, d, n = w_ada.shape
    b = c.shape[0]
    tn = 1024 if n % 1024 == 0 else n
    return pl.pallas_call(
        _mod_kernel,
        out_shape=jax.ShapeDtypeStruct((depth, b, n), _F32),
        grid=(depth, n // tn),
        in_specs=[
            pl.BlockSpec((b, d), lambda l, j: (0, 0)),
            pl.BlockSpec((None, d, tn), lambda l, j: (l, 0, j)),
            pl.BlockSpec((None, 1, tn), lambda l, j: (l, 0, j)),
        ],
        out_specs=pl.BlockSpec((None, b, tn), lambda l, j: (l, 0, j)),
        compiler_params=_cparams(2),
        name="adaln_mod",
    )(c, w_ada, b_ada.reshape(depth, 1, n))


def _norm_modulate(x, g, sc, sh):
    ms = jnp.mean(x * x, axis=-1, keepdims=True)
    r = x * lax.rsqrt(ms + NORM_EPS)
    return (r * g) * (1.0 + sc) + sh


def _proj_layout(d, hs, hd, hi):
    sizes = [("ga", d // LANES), ("gb", d // LANES),
             ("qa", hs), ("ka", hs), ("va", hs),
             ("qb", hd), ("kb", hd), ("vb", hd),
             ("qi", hi * IDX_DIM // LANES)]
    spb = PROJ_BLOCKS_PER_STEP
    sizes.append(("misc", spb * CHUNKS_PER_BLOCK))
    off, lay = 0, {"spb": spb}
    for name, n in sizes:
        assert n % (spb * CHUNKS_PER_BLOCK) == 0, (name, n)
        lay[name] = (off, n)
        off += n
    lay["total"] = off
    return lay


def _rope128(t, cos, sin):
    return t * cos + pltpu.roll(t, HEAD_DIM // 2, axis=1) * sin


def _rope64(t, cos, sin_lo, sin_hi):
    half = IDX_DIM // 2
    return (t * cos + pltpu.roll(t, LANES - half, axis=1) * sin_lo
            + pltpu.roll(t, half, axis=1) * sin_hi)


def _in_proj_kernel(x_ref, g_ref, sc_ref, sh_ref, w_ref, c128_ref, s128_ref,
                    c64_ref, s64lo_ref, s64hi_ref, o_ref, wi_ref, h_ref, *, lay, idx_scale, tm):
    j = pl.program_id(1)

    @pl.when(j == 0)
    def _():
        h = _norm_modulate(x_ref[...], g_ref[...], sc_ref[...], sh_ref[...])
        h_ref[...] = h.astype(_BF16)

    spb = lay["spb"]
    step_chunks = spb * CHUNKS_PER_BLOCK
    blk = lambda name: (lay[name][0] // step_chunks, (lay[name][0] + lay[name][1]) // step_chunks)
    qa_scale = (HEAD_DIM ** -0.5) * LOG2E
    qb_scale = (HEAD_DIM ** -0.5) * LOG2E
    rc = min(ROW_CHUNK, tm)

    def run(epilogue, nsub=spb):
        for sub in range(nsub):
            for r in range(tm // rc):
                rows = slice(r * rc, (r + 1) * rc)
                acc = jnp.dot(h_ref[rows, :], w_ref[:, sub * PROJ_TN:(sub + 1) * PROJ_TN],
                              preferred_element_type=_F32)
                epilogue(acc, rows, sub * CHUNKS_PER_BLOCK)

    def per_chunk(fn):
        def epilogue(acc, rows, c0):
            for c in range(CHUNKS_PER_BLOCK):
                o_ref[c0 + c, rows, :] = fn(acc[:, c * LANES:(c + 1) * LANES], rows).astype(_BF16)
        return epilogue

    def in_group(*names):
        cond = None
        for nm in names:
            lo, hi = blk(nm)
            t = jnp.logical_and(j >= lo, j < hi)
            cond = t if cond is None else jnp.logical_or(cond, t)
        return cond

    rope128 = lambda t, rows: _rope128(t, c128_ref[rows, :], s128_ref[rows, :])
    rope64 = lambda t, rows: _rope64(t, c64_ref[rows, :], s64lo_ref[rows, :], s64hi_ref[rows, :])

    @pl.when(in_group("ga", "gb"))
    def _():
        run(per_chunk(lambda t, rows: jax.nn.sigmoid(t)))

    @pl.when(in_group("qa"))
    def _():
        run(per_chunk(lambda t, rows: t * qa_scale))

    @pl.when(in_group("ka", "va", "vb"))
    def _():
        run(per_chunk(lambda t, rows: t))

    @pl.when(in_group("qb"))
    def _():
        run(per_chunk(lambda t, rows: rope128(t, rows) * qb_scale))

    @pl.when(in_group("kb"))
    def _():
        run(per_chunk(rope128))

    @pl.when(in_group("qi"))
    def _():
        run(per_chunk(rope64))

    @pl.when(in_group("misc"))
    def _():
        def epilogue(acc, rows, c0):
            for c in range(2):
                o_ref[c, rows, :] = rope64(acc[:, c * LANES:(c + 1) * LANES], rows).astype(_BF16)
            o_ref[2, rows, :] = acc[:, 2 * LANES:3 * LANES].astype(_BF16)
            wi_ref[rows, :] = acc[:, 2 * LANES:3 * LANES] * idx_scale
        run(epilogue, nsub=1)
        o_ref[3:step_chunks] = jnp.zeros((step_chunks - 3, tm, LANES), _BF16)


def _in_proj(x2, g, sc, sh, w, layer, tabs, lay, hi, seq, tm):
    m, d = x2.shape
    step_chunks = lay["spb"] * CHUNKS_PER_BLOCK
    nblk = lay["total"] // step_chunks
    tps = seq // tm
    row_spec = pl.BlockSpec((tm, LANES), lambda i, j: (i, 0))
    mod_spec = pl.BlockSpec((None, 1, d), lambda i, j: (i // tps, 0, 0))
    kern = functools.partial(_in_proj_kernel, lay=lay, tm=tm,
                             idx_scale=(IDX_DIM ** -0.5) * (hi ** -0.5))
    return pl.pallas_call(
        kern,
        out_shape=(jax.ShapeDtypeStruct((lay["total"], m, LANES), _BF16),
                   jax.ShapeDtypeStruct((m, LANES), _F32)),
        grid=(m // tm, nblk),
        in_specs=[
            pl.BlockSpec((tm, d), lambda i, j: (i, 0)),
            pl.BlockSpec((None, 1, d), lambda i, j: (layer, 0, 0)),
            mod_spec, mod_spec,
            pl.BlockSpec((None, d, step_chunks * LANES), lambda i, j: (layer, 0, j)),
            row_spec, row_spec, row_spec, row_spec, row_spec,
        ],
        out_specs=(pl.BlockSpec((step_chunks, tm, LANES), lambda i, j: (j, i, 0)),
                   pl.BlockSpec((tm, LANES), lambda i, j: (i, 0))),
        scratch_shapes=[pltpu.VMEM((tm, d), _BF16)],
        compiler_params=_cparams(2),
        name="in_proj",
    )(x2, g, sc, sh, w, *tabs)


def _add_lane_replicated(x, rep):
    n = x.shape[1] // LANES
    return jnp.concatenate([x[:, c * LANES:(c + 1) * LANES] + rep for c in range(n)], axis=1)


def _sb_kernel(q_ref, k_ref, v_ref, tri_ref, o_ref, *, t, nh, nq):
    qi = pl.program_id(2)

    def variant(n):
        past = (lax.broadcasted_iota(jnp.int32, (t, t), 1)
                < lax.broadcasted_iota(jnp.int32, (t, t), 0))

        def front(hh, kb):
            k = k_ref[hh, kb * t:(kb + 1) * t, :]
            z = lax.dot_general(q_ref[hh], k, _NT, preferred_element_type=_F32)
            sp = jnp.maximum(z, 0.0) + jnp.log(1.0 + jnp.exp2(-jnp.abs(z))) * LOG2E
            spm = (jnp.where(past, sp, 0.0) if kb == n else sp).astype(_BF16)
            cum = jnp.dot(spm, tri_ref[...], preferred_element_type=_F32)
            total = cum[:, 0:1] - spm[:, 0:1].astype(_F32)
            return z - sp, cum, total

        def back(hh, kb, zs, cum, carry):
            w = jnp.exp2(zs + _add_lane_replicated(cum, carry))
            if kb == n:
                w = jnp.where(past, w, 0.0)
            v = v_ref[hh, kb * t:(kb + 1) * t, :]
            return jnp.dot(w.astype(_BF16), v, preferred_element_type=_F32)

        items = [(hh, kb) for hh in range(nh) for kb in range(n, -1, -1)]
        ahead = front(*items[0])
        for idx, (hh, kb) in enumerate(items):
            zs, cum, total = ahead
            if idx + 1 < len(items):
                ahead = front(*items[idx + 1])
            if kb == n:
                acc = jnp.zeros((t, HEAD_DIM), _F32)
                carry = jnp.zeros((t, LANES), _F32)
            acc = acc + back(hh, kb, zs, cum, carry)
            carry = carry + jnp.broadcast_to(total, (t, LANES))
            if kb == 0:
                o_ref[:, hh * HEAD_DIM:(hh + 1) * HEAD_DIM] = acc.astype(_BF16)

    for n in range(nq):
        pl.when(qi == n)(functools.partial(variant, n))


def _sb_attention(proj, lay, batch, seq, hs, t, nh):
    m = batch * seq
    nq = seq // t
    qa, ka, va = (lay[n][0] // nh for n in ("qa", "ka", "va"))
    assert hs % nh == 0 and all(lay[n][0] % nh == 0 for n in ("qa", "ka", "va"))
    r = lax.broadcasted_iota(jnp.int32, (t, t), 0)
    c = lax.broadcasted_iota(jnp.int32, (t, t), 1)
    tri = jnp.where(r > c, -1.0, 0.0).astype(_BF16)
    return pl.pallas_call(
        functools.partial(_sb_kernel, t=t, nh=nh, nq=nq),
        out_shape=jax.ShapeDtypeStruct((m, hs * HEAD_DIM), _BF16),
        grid=(batch, hs // nh, nq),
        in_specs=[
            pl.BlockSpec((nh, t, HEAD_DIM), lambda b, h, i: (qa + h, b * nq + i, 0)),
            pl.BlockSpec((nh, seq, HEAD_DIM), lambda b, h, i: (ka + h, b, 0)),
            pl.BlockSpec((nh, seq, HEAD_DIM), lambda b, h, i: (va + h, b, 0)),
            pl.BlockSpec((t, t), lambda b, h, i: (0, 0)),
        ],
        out_specs=pl.BlockSpec((t, nh * HEAD_DIM), lambda b, h, i: (b * nq + i, h)),
        compiler_params=_cparams(3),
        name="sb_attn",
    )(proj, proj, proj, tri)


def _sortable(score):
    u = lax.bitcast_convert_type(score, jnp.int32)
    return u ^ (lax.shift_right_arithmetic(u, 31) & 0x7FFFFFFF)


def _dsa_kernel(qi_ref, ke_ref, wi_ref, qb_ref, kb_ref, vb_ref, lowtri_ref, o_ref,
                wt_ref, vt_ref, score_ref, key_ref, k16_ref, bias_ref, *, tq, seq, hi, hd, topk,
                nbuckets):
    i = pl.program_id(1)
    npairs = hi // 2
    tchunk = 256 if seq % 256 == 0 else LANES

    @pl.when(i == 0)
    def _():
        def xpose(h, c):
            for r in range(seq // tchunk):
                blk = vb_ref[h, r * tchunk:(r + 1) * tchunk, :].astype(_F32)
                vt_ref[h, :, r * tchunk:(r + 1) * tchunk] = blk.T.astype(_BF16)
            return c
        lax.fori_loop(0, hd, xpose, 0)

    wt_ref[...] = wi_ref[...].T

    def run(width):
        key_pos = lax.broadcasted_iota(jnp.int32, (width, tq), 0)
        q_pos = i * tq + lax.broadcasted_iota(jnp.int32, (width, tq), 1)
        visible = key_pos <= q_pos

        def pair(p):
            qp = qi_ref[p]
            d_even = lax.dot_general(ke_ref[0, :width, :], qp, _NT, preferred_element_type=_F32)
            d_odd = lax.dot_general(ke_ref[1, :width, :], qp, _NT, preferred_element_type=_F32)
            return (wt_ref[pl.ds(2 * p, 1), :] * jnp.maximum(d_even, 0.0)
                    + wt_ref[pl.ds(2 * p + 1, 1), :] * jnp.maximum(d_odd, 0.0))

        score_ref[:width, :] = jnp.zeros((width, tq), _F32)
        group = _pick(npairs, (4, 2, 1))

        def pairs(g, c):
            for u in range(group):
                score_ref[:width, :] += pair(g * group + u)
            return c

        lax.fori_loop(0, npairs // group, pairs, 0)
        key_ref[:width, :] = jnp.where(visible, _sortable(score_ref[:width, :]), INT_MIN)

        def count_ge16(cand):
            hit = jnp.where(k16_ref[:width, :] >= cand.astype(jnp.int16), jnp.int16(1), jnp.int16(0))
            parts = [hit[r * PACK16:(r + 1) * PACK16, :] for r in range(width // PACK16)]
            while len(parts) > 1:
                odd = parts[-1:] if len(parts) % 2 else []
                parts = [a + b for a, b in zip(parts[0::2], parts[1::2])] + odd
            return jnp.sum(parts[0].astype(jnp.int32), axis=0, keepdims=True)

        def search16(need):
            prefix = jnp.where(count_ge16(jnp.zeros((1, tq), jnp.int32)) >= need, 0, I16_MIN)

            def step(b, prefix):
                cand = prefix + lax.shift_left(jnp.int32(1), 14 - b)
                return jnp.where(count_ge16(cand) >= need, cand, prefix)

            return lax.fori_loop(0, 15, step, prefix.astype(jnp.int32))

        key_hi = lax.shift_right_arithmetic(key_ref[:width, :], 16)
        k16_ref[:width, :] = key_hi.astype(jnp.int16)
        t_hi = search16(jnp.full((1, tq), topk, jnp.int32))
        n_above = jnp.where(t_hi < I16_MAX, count_ge16(jnp.minimum(t_hi + 1, I16_MAX)), 0)
        key = key_ref[:width, :]
        key_lo = (key & 0xFFFF) + I16_MIN
        tied = lax.shift_right_arithmetic(key, 16) == t_hi
        k16_ref[:width, :] = jnp.where(tied, key_lo, I16_MIN).astype(jnp.int16)
        t_lo = search16(topk - n_above)
        thr = t_hi * 65536 + (t_lo - I16_MIN)
        thr = jnp.maximum(thr, INT_MIN + 1)
        selected = key_ref[:width, :] >= thr
        bias_ref[:width, :] = jnp.where(selected, 0.0, NEG_BIG)
        n_selected = jnp.sum(jnp.where(selected, 1.0, 0.0), axis=0, keepdims=True)

        @pl.when(jnp.max(n_selected) > topk)
        def _():
            key = key_ref[:width, :]
            above = key > thr
            need = topk - jnp.sum(jnp.where(above, 1.0, 0.0), axis=0, keepdims=True)

            def block(r, before):
                rows = pl.ds(pl.multiple_of(r * tq, tq), tq)
                kblk = key_ref[rows, :]
                tie = kblk == thr
                within = jnp.dot(lowtri_ref[...], jnp.where(tie, 1.0, 0.0).astype(_BF16),
                                 preferred_element_type=_F32)
                keep = jnp.logical_and(tie, within + before <= need)
                bias_ref[rows, :] = jnp.where(jnp.logical_or(kblk > thr, keep), 0.0, NEG_BIG)
                return before + within[tq - 1:tq, :]

            lax.fori_loop(0, width // tq, block, jnp.zeros((1, tq), _F32))

        def logits(h):
            s = lax.dot_general(kb_ref[h, :width, :], qb_ref[h], _NT, preferred_element_type=_F32)
            return s + bias_ref[:width, :]

        def finish(h, s):
            mx = jnp.max(s, axis=0, keepdims=True)
            p = jnp.exp2(s - mx)
            den = jnp.sum(p, axis=0, keepdims=True)
            ot = jnp.dot(vt_ref[h, :, :width], p.astype(_BF16), preferred_element_type=_F32)
            o_ref[h] = (ot / den).T.astype(_BF16)

        group_h = _pick(hd, (4, 2, 1))

        def heads(g, c):
            h0 = g * group_h
            s_next = logits(h0)
            for u in range(group_h):
                s_cur = s_next
                if u + 1 < group_h:
                    s_next = logits(h0 + u + 1)
                finish(h0 + u, s_cur)
            return c

        lax.fori_loop(0, hd // group_h, heads, 0)

    nq = seq // tq
    per = nq // nbuckets
    bucket = i // per
    for bi in range(nbuckets):
        pl.when(bucket == bi)(functools.partial(run, (bi + 1) * per * tq))


def _dsa_attention(proj, wi, lay, batch, seq, hi, hd, tq):
    m = batch * seq
    nq = seq // tq
    topk = min(TOPK_MAX, seq // 4)
    nbuckets = 4 if nq % 4 == 0 else 1
    npairs = hi // 2
    qi_b = lay["qi"][0] // npairs
    ke_b = lay["misc"][0] // 2
    qb_b, kb_b, vb_b = (lay[n][0] // hd for n in ("qb", "kb", "vb"))
    assert lay["qi"][0] % npairs == 0 and lay["misc"][0] % 2 == 0
    assert all(lay[n][0] % hd == 0 for n in ("qb", "kb", "vb"))
    kern = functools.partial(_dsa_kernel, tq=tq, seq=seq, hi=hi, hd=hd, topk=topk,
                             nbuckets=nbuckets)
    r = lax.broadcasted_iota(jnp.int32, (tq, tq), 0)
    c = lax.broadcasted_iota(jnp.int32, (tq, tq), 1)
    lowtri = jnp.where(c <= r, 1.0, 0.0).astype(_BF16)
    return pl.pallas_call(
        kern,
        out_shape=jax.ShapeDtypeStruct((hd, m, HEAD_DIM), _BF16),
        grid=(batch, nq),
        in_specs=[
            pl.BlockSpec((npairs, tq, LANES), lambda b, i: (qi_b, b * nq + i, 0)),
            pl.BlockSpec((2, seq, LANES), lambda b, i: (ke_b, b, 0)),
            pl.BlockSpec((tq, LANES), lambda b, i: (b * nq + i, 0)),
            pl.BlockSpec((hd, tq, HEAD_DIM), lambda b, i: (qb_b, b * nq + i, 0)),
            pl.BlockSpec((hd, seq, HEAD_DIM), lambda b, i: (kb_b, b, 0)),
            pl.BlockSpec((hd, seq, HEAD_DIM), lambda b, i: (vb_b, b, 0)),
            pl.BlockSpec((tq, tq), lambda b, i: (0, 0)),
        ],
        out_specs=pl.BlockSpec((hd, tq, HEAD_DIM), lambda b, i: (0, b * nq + i, 0)),
        scratch_shapes=[pltpu.VMEM((LANES, tq), _F32),
                        pltpu.VMEM((hd, HEAD_DIM, seq), _BF16),
                        pltpu.VMEM((seq, tq), _F32),
                        pltpu.VMEM((seq, tq), jnp.int32),
                        pltpu.VMEM((seq, tq), jnp.int16),
                        pltpu.VMEM((seq, tq), _F32)],
        compiler_params=_cparams(2),
        name="dsa_attn",
    )(proj, proj, wi, proj, proj, proj, lowtri)


def _mix_out_kernel(ya_ref, yb_ref, wa_ref, wb_ref, ga_ref, gb_ref, wo_ref, x_ref, gt_ref,
                    o_ref, mg_ref, *, hd, tm):
    rc = min(ROW_CHUNK, tm)
    for r in range(tm // rc):
        rows = slice(r * rc, (r + 1) * rc)
        yb = jnp.concatenate([yb_ref[h, rows, :] for h in range(hd)], axis=1)
        a = jnp.dot(ya_ref[rows, :], wa_ref[...], preferred_element_type=_F32)
        b = jnp.dot(yb, wb_ref[...], preferred_element_type=_F32)
        for c in range(a.shape[1] // LANES):
            sl = slice(c * LANES, (c + 1) * LANES)
            mg_ref[rows, sl] = (ga_ref[c, rows, :].astype(_F32) * a[:, sl]
                                + gb_ref[c, rows, :].astype(_F32) * b[:, sl]).astype(_BF16)
    for r in range(tm // rc):
        rows = slice(r * rc, (r + 1) * rc)
        acc = jnp.dot(mg_ref[rows, :], wo_ref[...], preferred_element_type=_F32)
        o_ref[rows, :] = x_ref[rows, :] + (1.0 + gt_ref[...]) * acc


def _mix_out(ya, yb, wa, wb, wo, layer, proj, x2, gt, lay, seq, tm):
    m, ka = ya.shape
    hd = yb.shape[0]
    d = wo.shape[2]
    nchunks = d // LANES
    tps = seq // tm
    assert lay["ga"][0] % nchunks == 0 and lay["gb"][0] % nchunks == 0
    gate = lambda name: pl.BlockSpec((nchunks, tm, LANES),
                                     lambda i: (lay[name][0] // nchunks, i, 0))
    resident = lambda rows: pl.BlockSpec((None, rows, d), lambda i: (layer, 0, 0),
                                         pipeline_mode=pl.Buffered(1))
    return pl.pallas_call(
        functools.partial(_mix_out_kernel, hd=hd, tm=tm),
        out_shape=jax.ShapeDtypeStruct((m, d), _F32),
        grid=(m // tm,),
        in_specs=[
            pl.BlockSpec((tm, ka), lambda i: (i, 0)),
            pl.BlockSpec((hd, tm, HEAD_DIM), lambda i: (0, i, 0)),
            resident(ka), resident(hd * HEAD_DIM),
            gate("ga"), gate("gb"),
            resident(d),
            pl.BlockSpec((tm, d), lambda i: (i, 0)),
            pl.BlockSpec((None, 1, d), lambda i: (i // tps, 0, 0)),
        ],
        out_specs=pl.BlockSpec((tm, d), lambda i: (i, 0)),
        scratch_shapes=[pltpu.VMEM((tm, d), _BF16)],
        compiler_params=_cparams(1),
        name="mix_out",
    )(ya, yb, wa, wb, proj, proj, wo, x2, gt)


def _out_res_kernel(a_ref, w_ref, x_ref, gt_ref, o_ref):
    acc = jnp.dot(a_ref[...], w_ref[...], preferred_element_type=_F32)
    o_ref[...] = x_ref[...] + (1.0 + gt_ref[...]) * acc


def _out_res(a, w, layer, x2, gt, seq, tm, tn):
    m, k = a.shape
    d = w.shape[2]
    tps = seq // tm
    return pl.pallas_call(
        _out_res_kernel,
        out_shape=jax.ShapeDtypeStruct((m, d), _F32),
        grid=(m // tm, d // tn),
        in_specs=[
            pl.BlockSpec((tm, k), lambda i, j: (i, 0)),
            pl.BlockSpec((None, k, tn), lambda i, j: (layer, 0, j)),
            pl.BlockSpec((tm, tn), lambda i, j: (i, j)),
            pl.BlockSpec((None, 1, tn), lambda i, j: (i // tps, 0, j)),
        ],
        out_specs=pl.BlockSpec((tm, tn), lambda i, j: (i, j)),
        compiler_params=_cparams(2),
        name="out_res",
    )(a, w, x2, gt)


def _shift_rows(u, k, prev):
    rolled = pltpu.roll(u, k, axis=0)
    first = lax.broadcasted_iota(jnp.int32, prev.shape, 0) < k
    head = jnp.where(first, pltpu.roll(prev, k, axis=0), rolled[:SUBLANES])
    return jnp.concatenate([head, rolled[SUBLANES:]], axis=0)


def _ffn_up_kernel(x_ref, g_ref, sc_ref, sh_ref, wa_ref, wb_ref, cwa_ref, cwb_ref, cba_ref,
                   cbb_ref, o_ref, h_ref, halo_ref, carry_ref, *, tm, tn, tps):
    i = pl.program_id(0)
    j = pl.program_id(1)
    rc = min(ROW_CHUNK, tm)

    @pl.when(j == 0)
    def _():
        h = _norm_modulate(x_ref[...], g_ref[...], sc_ref[...], sh_ref[...])
        h_ref[...] = h.astype(_BF16)

    @pl.when(i % tps == 0)
    def _():
        halo_ref[...] = jnp.zeros_like(halo_ref)

    @pl.when(i % tps != 0)
    def _():
        halo_ref[...] = carry_ref[j]

    def conv(u, prev, cw_ref, cb_ref):
        cw = cw_ref[...]
        u1 = _shift_rows(u, 1, prev)
        u2 = _shift_rows(u, 2, prev)
        return cb_ref[...] + (cw[0:1, :] * u2 + cw[1:2, :] * u1 + cw[2:3, :] * u)

    prev_a = halo_ref[:, 0:tn]
    prev_b = halo_ref[:, tn:2 * tn]
    for r in range(tm // rc):
        lo = r * rc
        hr = h_ref[lo:lo + rc, :]
        ua = jnp.dot(hr, wa_ref[...], preferred_element_type=_F32)
        ub = jnp.dot(hr, wb_ref[...], preferred_element_type=_F32)
        a = conv(ua, prev_a, cwa_ref, cba_ref)
        b = conv(ub, prev_b, cwb_ref, cbb_ref)
        o_ref[lo:lo + rc, :] = (a * jax.nn.sigmoid(a) * b).astype(_BF16)
        prev_a = ua[rc - SUBLANES:, :]
        prev_b = ub[rc - SUBLANES:, :]

    carry_ref[j] = jnp.concatenate([prev_a, prev_b], axis=1)


def _ffn_up(x2, g, sc, sh, w, cw, cb, layer, seq, tm, tn):
    m, d = x2.shape
    nblk = w.shape[2] // (2 * tn)
    tps = seq // tm
    mod_spec = pl.BlockSpec((None, 1, d), lambda i, j: (i // tps, 0, 0))
    halves = lambda rows: (pl.BlockSpec((None, rows, tn), lambda i, j: (layer, 0, j)),
                           pl.BlockSpec((None, rows, tn), lambda i, j: (layer, 0, nblk + j)))
    return pl.pallas_call(
        functools.partial(_ffn_up_kernel, tm=tm, tn=tn, tps=tps),
        out_shape=jax.ShapeDtypeStruct((m, nblk * tn), _BF16),
        grid=(m // tm, nblk),
        in_specs=[
            pl.BlockSpec((tm, d), lambda i, j: (i, 0)),
            pl.BlockSpec((None, 1, d), lambda i, j: (layer, 0, 0)),
            mod_spec, mod_spec,
            *halves(d), *halves(CONV_WIDTH), *halves(1),
        ],
        out_specs=pl.BlockSpec((tm, tn), lambda i, j: (i, j)),
        scratch_shapes=[pltpu.VMEM((tm, d), _BF16),
                        pltpu.VMEM((SUBLANES, 2 * tn), _F32),
                        pltpu.VMEM((nblk, SUBLANES, 2 * tn), _F32)],
        compiler_params=_cparams(2),
        name="ffn_up",
    )(x2, g, sc, sh, w, w, cw, cw, cb, cb)


def _final_norm_kernel(x_ref, g_ref, o_ref):
    x = x_ref[...]
    ms = jnp.mean(x * x, axis=-1, keepdims=True)
    o_ref[...] = (x * lax.rsqrt(ms + NORM_EPS)) * g_ref[...]


def _final_norm(x2, g, tm):
    m, d = x2.shape
    return pl.pallas_call(
        _final_norm_kernel,
        out_shape=jax.ShapeDtypeStruct((m, d), _F32),
        grid=(m // tm,),
        in_specs=[pl.BlockSpec((tm, d), lambda i: (i, 0)),
                  pl.BlockSpec((1, d), lambda i: (0, 0))],
        out_specs=pl.BlockSpec((tm, d), lambda i: (i, 0)),
        compiler_params=_cparams(1),
        name="final_norm",
    )(x2, g)


def _arrange_w_in(w_in, d, hs, hd, hi, misc_chunks):
    sw, dw, iw = hs * HEAD_DIM, hd * HEAD_DIM, hi * IDX_DIM
    sizes = (sw, sw, sw, dw, dw, dw, iw, IDX_DIM, hi, d, d)
    offs = [0]
    for s in sizes:
        offs.append(offs[-1] + s)
    qa, ka, va, qb, kb, vb, qi, ki, wi, ga, gb = (
        w_in[..., offs[n]:offs[n + 1]] for n in range(len(sizes)))
    z = lambda n: jnp.zeros(w_in.shape[:-1] + (n,), w_in.dtype)
    misc = jnp.concatenate([ki, z(LANES - IDX_DIM), z(LANES - IDX_DIM), ki,
                            wi, z(LANES - hi), z((misc_chunks - 3) * LANES)], axis=-1)
    return jnp.concatenate([ga, gb, qa, ka, va, qb, kb, vb, qi, misc], axis=-1).astype(_BF16)


def _rope_tables(positions):
    pos = positions.reshape(-1).astype(_F32)

    def cos_sin(dim):
        inv_freq = 1.0 / (ROPE_THETA ** (jnp.arange(0, dim, 2, dtype=_F32) / dim))
        ang = pos[:, None] * inv_freq
        return jnp.cos(ang), jnp.sin(ang)

    c, s = cos_sin(HEAD_DIM)
    c128 = jnp.concatenate([c, c], axis=-1)
    s128 = jnp.concatenate([-s, s], axis=-1)
    c, s = cos_sin(IDX_DIM)
    zero = jnp.zeros_like(s)
    c64 = jnp.concatenate([c, c, c, c], axis=-1)
    s64lo = jnp.concatenate([-s, zero, -s, zero], axis=-1)
    s64hi = jnp.concatenate([zero, s, zero, s], axis=-1)
    return c128, s128, c64, s64lo, s64hi


def _pick(n, prefs):
    for p in prefs:
        if n % p == 0:
            return p
    return n


def kernel(x, c, positions, w_in, w_a, w_b, w_o, w_ada, b_ada, g_mix, g_ffn, w_up, conv_w,
           conv_b, w_down, g_final):
    batch, seq, d = x.shape
    depth = w_in.shape[0]
    hs = w_a.shape[1] // HEAD_DIM
    hd = w_b.shape[1] // HEAD_DIM
    rest = w_in.shape[2] - 3 * hs * HEAD_DIM - 3 * hd * HEAD_DIM - 2 * d - IDX_DIM
    hi = rest // (IDX_DIM + 1)
    dff = w_down.shape[1]
    m = batch * seq
    lay = _proj_layout(d, hs, hd, hi)

    tm = _pick(seq, (1024, 512, 256, 128))
    tm_mix = _pick(seq, (512, 256, 128))
    t_sb = _pick(seq, (256, 128))
    nh_sb = 2
    tq_dsa = _pick(seq, (256, 128))
    tn_ffn = _pick(dff, (512, 256, 128))
    tn_out = _pick(d, (512, 256, 128))

    w_in_r = _arrange_w_in(w_in, d, hs, hd, hi, lay["misc"][1])
    w_a16, w_b16, w_o16, w_up16, w_down16 = (t.astype(_BF16)
                                             for t in (w_a, w_b, w_o, w_up, w_down))
    tabs = _rope_tables(positions)
    mod = _modulation(c, w_ada, b_ada).reshape(depth, batch, N_MOD, 1, d)
    g_mix3, g_ffn3 = g_mix.reshape(depth, 1, d), g_ffn.reshape(depth, 1, d)
    conv_b3 = conv_b.reshape(depth, 1, -1)

    x2 = x.reshape(m, d)
    for l in range(depth):
        sh1, sc1, gt1, sh2, sc2, gt2 = (mod[l, :, n] for n in range(N_MOD))
        proj, wi = _in_proj(x2, g_mix3, sc1, sh1, w_in_r, l, tabs, lay, hi, seq, tm)
        ya = _sb_attention(proj, lay, batch, seq, hs, t_sb, nh_sb)
        yb = _dsa_attention(proj, wi, lay, batch, seq, hi, hd, tq_dsa)
        x2 = _mix_out(ya, yb, w_a16, w_b16, w_o16, l, proj, x2, gt1, lay, seq, tm_mix)
        g = _ffn_up(x2, g_ffn3, sc2, sh2, w_up16, conv_w, conv_b3, l, seq, tm, tn_ffn)
        x2 = _out_res(g, w_down16, l, x2, gt2, seq, tm, tn_out)
    out = _final_norm(x2, g_final.reshape(1, d), tm)
    return out.reshape(batch, seq, d)
```

```python
import functools
import math

import jax
import jax.numpy as jnp
from jax import lax
from jax.experimental import pallas as pl
from jax.experimental.pallas import tpu as pltpu

HEAD_DIM = 128
IDX_DIM = 64
TOPK_MAX = 256
CONV_WIDTH = 3
ROPE_THETA = 10000.0
NORM_EPS = 1e-6
N_MOD = 6
LANES = 128
SUBLANES = 8
CHUNKS_PER_BLOCK = 4
PROJ_TN = LANES * CHUNKS_PER_BLOCK
PROJ_BLOCKS_PER_STEP = 1
VMEM_LIMIT = 56 * 1024 * 1024
INT_MIN = -2 ** 31
I16_MIN, I16_MAX = -2 ** 15, 2 ** 15 - 1
PACK16 = 16
NEG_BIG = -1e30
ROW_CHUNK = 256
LOG2E = math.log2(math.e)

_F32 = jnp.float32
_BF16 = jnp.bfloat16
_NT = (((1,), (1,)), ((), ()))


def _cparams(n_axes):
    return pltpu.CompilerParams(dimension_semantics=("arbitrary",) * n_axes,
                                vmem_limit_bytes=VMEM_LIMIT)


def _mod_kernel(c_ref, w_ref, b_ref, o_ref):
    c = c_ref[...]
    c_act = (c * jax.nn.sigmoid(c)).astype(_BF16)
    acc = jnp.dot(c_act, w_ref[...].astype(_BF16), preferred_element_type=_F32)
    o_ref[...] = acc + b_ref[...]


def _modulation(c, w_ada, b_ada):
    depth, d, n = w_ada.shape
    b = c.shape[0]
    tn = 1024 if n % 1024 == 0 else n
    return pl.pallas_call(
        _mod_kernel,
        out_shape=jax.ShapeDtypeStruct((depth, b, n), _F32),
        grid=(depth, n // tn),
        in_specs=[
            pl.BlockSpec((b, d), lambda l, j: (0, 0)),
            pl.BlockSpec((None, d, tn), lambda l, j: (l, 0, j)),
            pl.BlockSpec((None, 1, tn), lambda l, j: (l, 0, j)),
        ],
        out_specs=pl.BlockSpec((None, b, tn), lambda l, j: (l, 0, j)),
        compiler_params=_cparams(2),
        name="adaln_mod",
    )(c, w_ada, b_ada.reshape(depth, 1, n))


def _norm_modulate(x, g, sc, sh):
    ms = jnp.mean(x * x, axis=-1, keepdims=True)
    r = x * lax.rsqrt(ms + NORM_EPS)
    return (r * g) * (1.0 + sc) + sh


def _proj_layout(d, hs, hd, hi):
    sizes = [("ga", d // LANES), ("gb", d // LANES),
             ("qa", hs), ("ka", hs), ("va", hs),
             ("qb", hd), ("kb", hd), ("vb", hd),
             ("qi", hi * IDX_DIM // LANES)]
    spb = PROJ_BLOCKS_PER_STEP
    sizes.append(("misc", spb * CHUNKS_PER_BLOCK))
    off, lay = 0, {"spb": spb}
    for name, n in sizes:
        assert n % (spb * CHUNKS_PER_BLOCK) == 0, (name, n)
        lay[name] = (off, n)
        off += n
    lay["total"] = off
    return lay


def _rope128(t, cos, sin):
    return t * cos + pltpu.roll(t, HEAD_DIM // 2, axis=1) * sin


def _rope64(t, cos, sin_lo, sin_hi):
    half = IDX_DIM // 2
    return (t * cos + pltpu.roll(t, LANES - half, axis=1) * sin_lo
            + pltpu.roll(t, half, axis=1) * sin_hi)


def _in_proj_kernel(x_ref, g_ref, sc_ref, sh_ref, w_ref, c128_ref, s128_ref,
                    c64_ref, s64lo_ref, s64hi_ref, o_ref, wi_ref, h_ref, *, lay, idx_scale, tm):
    j = pl.program_id(1)

    @pl.when(j == 0)
    def _():
        h = _norm_modulate(x_ref[...], g_ref[...], sc_ref[...], sh_ref[...])
        h_ref[...] = h.astype(_BF16)

    spb = lay["spb"]
    step_chunks = spb * CHUNKS_PER_BLOCK
    blk = lambda name: (lay[name][0] // step_chunks, (lay[name][0] + lay[name][1]) // step_chunks)
    qa_scale = (HEAD_DIM ** -0.5) * LOG2E
    qb_scale = (HEAD_DIM ** -0.5) * LOG2E
    rc = min(ROW_CHUNK, tm)

    def run(epilogue, nsub=spb):
        for sub in range(nsub):
            for r in range(tm // rc):
                rows = slice(r * rc, (r + 1) * rc)
                acc = jnp.dot(h_ref[rows, :], w_ref[:, sub * PROJ_TN:(sub + 1) * PROJ_TN],
                              preferred_element_type=_F32)
                epilogue(acc, rows, sub * CHUNKS_PER_BLOCK)

    def per_chunk(fn):
        def epilogue(acc, rows, c0):
            for c in range(CHUNKS_PER_BLOCK):
                o_ref[c0 + c, rows, :] = fn(acc[:, c * LANES:(c + 1) * LANES], rows).astype(_BF16)
        return epilogue

    def in_group(*names):
        cond = None
        for nm in names:
            lo, hi = blk(nm)
            t = jnp.logical_and(j >= lo, j < hi)
            cond = t if cond is None else jnp.logical_or(cond, t)
        return cond

    rope128 = lambda t, rows: _rope128(t, c128_ref[rows, :], s128_ref[rows, :])
    rope64 = lambda t, rows: _rope64(t, c64_ref[rows, :], s64lo_ref[rows, :], s64hi_ref[rows, :])

    @pl.when(in_group("ga", "gb"))
    def _():
        run(per_chunk(lambda t, rows: jax.nn.sigmoid(t)))

    @pl.when(in_group("qa"))
    def _():
        run(per_chunk(lambda t, rows: t * qa_scale))

    @pl.when(in_group("ka", "va", "vb"))
    def _():
        run(per_chunk(lambda t, rows: t))

    @pl.when(in_group("qb"))
    def _():
        run(per_chunk(lambda t, rows: rope128(t, rows) * qb_scale))

    @pl.when(in_group("kb"))
    def _():
        run(per_chunk(rope128))

    @pl.when(in_group("qi"))
    def _():
        run(per_chunk(rope64))

    @pl.when(in_group("misc"))
    def _():
        def epilogue(acc, rows, c0):
            for c in range(2):
                o_ref[c, rows, :] = rope64(acc[:, c * LANES:(c + 1) * LANES], rows).astype(_BF16)
            o_ref[2, rows, :] = acc[:, 2 * LANES:3 * LANES].astype(_BF16)
            wi_ref[rows, :] = acc[:, 2 * LANES:3 * LANES] * idx_scale
        run(epilogue, nsub=1)
        o_ref[3:step_chunks] = jnp.zeros((step_chunks - 3, tm, LANES), _BF16)


def _in_proj(x2, g, sc, sh, w, layer, tabs, lay, hi, seq, tm):
    m, d = x2.shape
    step_chunks = lay["spb"] * CHUNKS_PER_BLOCK
    nblk = lay["total"] // step_chunks
    tps = seq // tm
    row_spec = pl.BlockSpec((tm, LANES), lambda i, j: (i, 0))
    mod_spec = pl.BlockSpec((None, 1, d), lambda i, j: (i // tps, 0, 0))
    kern = functools.partial(_in_proj_kernel, lay=lay, tm=tm,
                             idx_scale=(IDX_DIM ** -0.5) * (hi ** -0.5))
    return pl.pallas_call(
        kern,
        out_shape=(jax.ShapeDtypeStruct((lay["total"], m, LANES), _BF16),
                   jax.ShapeDtypeStruct((m, LANES), _F32)),
        grid=(m // tm, nblk),
        in_specs=[
            pl.BlockSpec((tm, d), lambda i, j: (i, 0)),
            pl.BlockSpec((None, 1, d), lambda i, j: (layer, 0, 0)),
            mod_spec, mod_spec,
            pl.BlockSpec((None, d, step_chunks * LANES), lambda i, j: (layer, 0, j)),
            row_spec, row_spec, row_spec, row_spec, row_spec,
        ],
        out_specs=(pl.BlockSpec((step_chunks, tm, LANES), lambda i, j: (j, i, 0)),
                   pl.BlockSpec((tm, LANES), lambda i, j: (i, 0))),
        scratch_shapes=[pltpu.VMEM((tm, d), _BF16)],
        compiler_params=_cparams(2),
        name="in_proj",
    )(x2, g, sc, sh, w, *tabs)


def _add_lane_replicated(x, rep):
    n = x.shape[1] // LANES
    return jnp.concatenate([x[:, c * LANES:(c + 1) * LANES] + rep for c in range(n)], axis=1)


def _sb_kernel(q_ref, k_ref, v_ref, tri_ref, o_ref, *, t, nh, nq, qps):
    qi = pl.program_id(2)

    def variant(n, rows):
        past = (lax.broadcasted_iota(jnp.int32, (t, t), 1)
                < lax.broadcasted_iota(jnp.int32, (t, t), 0))

        def front(hh, kb):
            k = k_ref[hh, kb * t:(kb + 1) * t, :]
            z = lax.dot_general(q_ref[hh, rows, :], k, _NT,
                                preferred_element_type=_F32)
            sp = jnp.maximum(z, 0.0) + jnp.log(1.0 + jnp.exp2(-jnp.abs(z))) * LOG2E
            spm = (jnp.where(past, sp, 0.0) if kb == n else sp).astype(_BF16)
            cum = jnp.dot(spm, tri_ref[...], preferred_element_type=_F32)
            total = cum[:, 0:1] - spm[:, 0:1].astype(_F32)
            return z - sp, cum, total

        def back(hh, kb, zs, cum, carry):
            w = jnp.exp2(zs + _add_lane_replicated(cum, carry))
            if kb == n:
                w = jnp.where(past, w, 0.0)
            v = v_ref[hh, kb * t:(kb + 1) * t, :]
            return jnp.dot(w.astype(_BF16), v, preferred_element_type=_F32)

        items = [(hh, kb) for hh in range(nh) for kb in range(n, -1, -1)]
        ahead = front(*items[0])
        for idx, (hh, kb) in enumerate(items):
            zs, cum, total = ahead
            if idx + 1 < len(items):
                ahead = front(*items[idx + 1])
            if kb == n:
                acc = jnp.zeros((t, HEAD_DIM), _F32)
                carry = jnp.zeros((t, LANES), _F32)
            acc = acc + back(hh, kb, zs, cum, carry)
            carry = carry + jnp.broadcast_to(total, (t, LANES))
            if kb == 0:
                o_ref[rows, hh * HEAD_DIM:(hh + 1) * HEAD_DIM] = acc.astype(_BF16)

    def step(m):
        for sub in range(qps):
            variant(m * qps + sub, slice(sub * t, (sub + 1) * t))

    for m in range(nq // qps):
        pl.when(qi == m)(functools.partial(step, m))


def _sb_attention(proj, lay, batch, seq, hs, t, nh):
    m = batch * seq
    nq = seq // t
    qps = _pick(nq, (2, 1))
    ns = nq // qps
    qa, ka, va = (lay[n][0] // nh for n in ("qa", "ka", "va"))
    assert hs % nh == 0 and all(lay[n][0] % nh == 0 for n in ("qa", "ka", "va"))
    r = lax.broadcasted_iota(jnp.int32, (t, t), 0)
    c = lax.broadcasted_iota(jnp.int32, (t, t), 1)
    tri = jnp.where(r > c, -1.0, 0.0).astype(_BF16)
    return pl.pallas_call(
        functools.partial(_sb_kernel, t=t, nh=nh, nq=nq, qps=qps),
        out_shape=jax.ShapeDtypeStruct((m, hs * HEAD_DIM), _BF16),
        grid=(batch, hs // nh, ns),
        in_specs=[
            pl.BlockSpec((nh, qps * t, HEAD_DIM), lambda b, h, i: (qa + h, b * ns + i, 0)),
            pl.BlockSpec((nh, seq, HEAD_DIM), lambda b, h, i: (ka + h, b, 0)),
            pl.BlockSpec((nh, seq, HEAD_DIM), lambda b, h, i: (va + h, b, 0)),
            pl.BlockSpec((t, t), lambda b, h, i: (0, 0)),
        ],
        out_specs=pl.BlockSpec((qps * t, nh * HEAD_DIM), lambda b, h, i: (b * ns + i, h)),
        compiler_params=_cparams(3),
        name="sb_attn",
    )(proj, proj, proj, tri)


def _sortable(score):
    u = lax.bitcast_convert_type(score, jnp.int32)
    return u ^ (lax.shift_right_arithmetic(u, 31) & 0x7FFFFFFF)


def _dsa_kernel(qi_ref, ke_ref, wi_ref, qb_ref, kb_ref, vb_ref, lowtri_ref, o_ref,
                wt_ref, vt_ref, score_ref, key_ref, k16_ref, bias_ref, *, tq, seq, hi, hd, topk,
                nbuckets):
    i = pl.program_id(1)
    npairs = hi // 2
    tchunk = 256 if seq % 256 == 0 else LANES

    @pl.when(i == 0)
    def _():
        def xpose(h, c):
            for r in range(seq // tchunk):
                blk = vb_ref[h, r * tchunk:(r + 1) * tchunk, :].astype(_F32)
                vt_ref[h, :, r * tchunk:(r + 1) * tchunk] = blk.T.astype(_BF16)
            return c
        lax.fori_loop(0, hd, xpose, 0)

    wt_ref[...] = wi_ref[...].T

    def run(width):
        key_pos = lax.broadcasted_iota(jnp.int32, (width, tq), 0)
        q_pos = i * tq + lax.broadcasted_iota(jnp.int32, (width, tq), 1)
        visible = key_pos <= q_pos

        def pair(p):
            qp = qi_ref[p]
            d_even = lax.dot_general(ke_ref[0, :width, :], qp, _NT, preferred_element_type=_F32)
            d_odd = lax.dot_general(ke_ref[1, :width, :], qp, _NT, preferred_element_type=_F32)
            return (wt_ref[pl.ds(2 * p, 1), :] * jnp.maximum(d_even, 0.0)
                    + wt_ref[pl.ds(2 * p + 1, 1), :] * jnp.maximum(d_odd, 0.0))

        score_ref[:width, :] = jnp.zeros((width, tq), _F32)
        group = _pick(npairs, (4, 2, 1))

        def pairs(g, c):
            for u in range(group):
                score_ref[:width, :] += pair(g * group + u)
            return c

        lax.fori_loop(0, npairs // group, pairs, 0)
        key_ref[:width, :] = jnp.where(visible, _sortable(score_ref[:width, :]), INT_MIN)

        def count_ge16(cand):
            hit = jnp.where(k16_ref[:width, :] >= cand.astype(jnp.int16), jnp.int16(1), jnp.int16(0))
            parts = [hit[r * PACK16:(r + 1) * PACK16, :] for r in range(width // PACK16)]
            while len(parts) > 1:
                odd = parts[-1:] if len(parts) % 2 else []
                parts = [a + b for a, b in zip(parts[0::2], parts[1::2])] + odd
            return jnp.sum(parts[0].astype(jnp.int32), axis=0, keepdims=True)

        def search16(need):
            prefix = jnp.where(count_ge16(jnp.zeros((1, tq), jnp.int32)) >= need, 0, I16_MIN)

            def step(b, prefix):
                cand = prefix + lax.shift_left(jnp.int32(1), 14 - b)
                return jnp.where(count_ge16(cand) >= need, cand, prefix)

            return lax.fori_loop(0, 15, step, prefix.astype(jnp.int32))

        key_hi = lax.shift_right_arithmetic(key_ref[:width, :], 16)
        k16_ref[:width, :] = key_hi.astype(jnp.int16)
        t_hi = search16(jnp.full((1, tq), topk, jnp.int32))
        n_above = jnp.where(t_hi < I16_MAX, count_ge16(jnp.minimum(t_hi + 1, I16_MAX)), 0)
        key = key_ref[:width, :]
        key_lo = (key & 0xFFFF) + I16_MIN
        tied = lax.shift_right_arithmetic(key, 16) == t_hi
        k16_ref[:width, :] = jnp.where(tied, key_lo, I16_MIN).astype(jnp.int16)
        t_lo = search16(topk - n_above)
        thr = t_hi * 65536 + (t_lo - I16_MIN)
        thr = jnp.maximum(thr, INT_MIN + 1)
        selected = key_ref[:width, :] >= thr
        bias_ref[:width, :] = jnp.where(selected, 0.0, NEG_BIG)
        n_selected = jnp.sum(jnp.where(selected, 1.0, 0.0), axis=0, keepdims=True)

        @pl.when(jnp.max(n_selected) > topk)
        def _():
            key = key_ref[:width, :]
            above = key > thr
            need = topk - jnp.sum(jnp.where(above, 1.0, 0.0), axis=0, keepdims=True)

            def block(r, before):
                rows = pl.ds(pl.multiple_of(r * tq, tq), tq)
                kblk = key_ref[rows, :]
                tie = kblk == thr
                within = jnp.dot(lowtri_ref[...], jnp.where(tie, 1.0, 0.0).astype(_BF16),
                                 preferred_element_type=_F32)
                keep = jnp.logical_and(tie, within + before <= need)
                bias_ref[rows, :] = jnp.where(jnp.logical_or(kblk > thr, keep), 0.0, NEG_BIG)
                return before + within[tq - 1:tq, :]

            lax.fori_loop(0, width // tq, block, jnp.zeros((1, tq), _F32))

        def logits(h):
            s = lax.dot_general(kb_ref[h, :width, :], qb_ref[h], _NT, preferred_element_type=_F32)
            return s + bias_ref[:width, :]

        def finish(h, s):
            mx = jnp.max(s, axis=0, keepdims=True)
            p = jnp.exp2(s - mx)
            den = jnp.sum(p, axis=0, keepdims=True)
            ot = jnp.dot(vt_ref[h, :, :width], p.astype(_BF16), preferred_element_type=_F32)
            o_ref[h] = (ot / den).T.astype(_BF16)

        group_h = _pick(hd, (4, 2, 1))

        def heads(g, c):
            h0 = g * group_h
            s_next = logits(h0)
            for u in range(group_h):
                s_cur = s_next
                if u + 1 < group_h:
                    s_next = logits(h0 + u + 1)
                finish(h0 + u, s_cur)
            return c

        lax.fori_loop(0, hd // group_h, heads, 0)

    nq = seq // tq
    per = nq // nbuckets
    bucket = i // per
    for bi in range(nbuckets):
        pl.when(bucket == bi)(functools.partial(run, (bi + 1) * per * tq))


def _dsa_attention(proj, wi, lay, batch, seq, hi, hd, tq):
    m = batch * seq
    nq = seq // tq
    topk = min(TOPK_MAX, seq // 4)
    nbuckets = 4 if nq % 4 == 0 else 1
    npairs = hi // 2
    qi_b = lay["qi"][0] // npairs
    ke_b = lay["misc"][0] // 2
    qb_b, kb_b, vb_b = (lay[n][0] // hd for n in ("qb", "kb", "vb"))
    assert lay["qi"][0] % npairs == 0 and lay["misc"][0] % 2 == 0
    assert all(lay[n][0] % hd == 0 for n in ("qb", "kb", "vb"))
    kern = functools.partial(_dsa_kernel, tq=tq, seq=seq, hi=hi, hd=hd, topk=topk,
                             nbuckets=nbuckets)
    r = lax.broadcasted_iota(jnp.int32, (tq, tq), 0)
    c = lax.broadcasted_iota(jnp.int32, (tq, tq), 1)
    lowtri = jnp.where(c <= r, 1.0, 0.0).astype(_BF16)
    return pl.pallas_call(
        kern,
        out_shape=jax.ShapeDtypeStruct((hd, m, HEAD_DIM), _BF16),
        grid=(batch, nq),
        in_specs=[
            pl.BlockSpec((npairs, tq, LANES), lambda b, i: (qi_b, b * nq + i, 0)),
            pl.BlockSpec((2, seq, LANES), lambda b, i: (ke_b, b, 0)),
            pl.BlockSpec((tq, LANES), lambda b, i: (b * nq + i, 0)),
            pl.BlockSpec((hd, tq, HEAD_DIM), lambda b, i: (qb_b, b * nq + i, 0)),
            pl.BlockSpec((hd, seq, HEAD_DIM), lambda b, i: (kb_b, b, 0)),
            pl.BlockSpec((hd, seq, HEAD_DIM), lambda b, i: (vb_b, b, 0)),
            pl.BlockSpec((tq, tq), lambda b, i: (0, 0)),
        ],
        out_specs=pl.BlockSpec((hd, tq, HEAD_DIM), lambda b, i: (0, b * nq + i, 0)),
        scratch_shapes=[pltpu.VMEM((LANES, tq), _F32),
                        pltpu.VMEM((hd, HEAD_DIM, seq), _BF16),
                        pltpu.VMEM((seq, tq), _F32),
                        pltpu.VMEM((seq, tq), jnp.int32),
                        pltpu.VMEM((seq, tq), jnp.int16),
                        pltpu.VMEM((seq, tq), _F32)],
        compiler_params=_cparams(2),
        name="dsa_attn",
    )(proj, proj, wi, proj, proj, proj, lowtri)


def _mix_out_kernel(ya_ref, yb_ref, wa_ref, wb_ref, ga_ref, gb_ref, wo_ref, x_ref, gt_ref,
                    o_ref, mg_ref, *, hd, tm):
    rc = min(ROW_CHUNK, tm)
    for r in range(tm // rc):
        rows = slice(r * rc, (r + 1) * rc)
        yb = jnp.concatenate([yb_ref[h, rows, :] for h in range(hd)], axis=1)
        a = jnp.dot(ya_ref[rows, :], wa_ref[...], preferred_element_type=_F32)
        b = jnp.dot(yb, wb_ref[...], preferred_element_type=_F32)
        for c in range(a.shape[1] // LANES):
            sl = slice(c * LANES, (c + 1) * LANES)
            mg_ref[rows, sl] = (ga_ref[c, rows, :].astype(_F32) * a[:, sl]
                                + gb_ref[c, rows, :].astype(_F32) * b[:, sl]).astype(_BF16)
    for r in range(tm // rc):
        rows = slice(r * rc, (r + 1) * rc)
        acc = jnp.dot(mg_ref[rows, :], wo_ref[...], preferred_element_type=_F32)
        o_ref[rows, :] = x_ref[rows, :] + (1.0 + gt_ref[...]) * acc


def _mix_out(ya, yb, wa, wb, wo, layer, proj, x2, gt, lay, seq, tm):
    m, ka = ya.shape
    hd = yb.shape[0]
    d = wo.shape[2]
    nchunks = d // LANES
    tps = seq // tm
    assert lay["ga"][0] % nchunks == 0 and lay["gb"][0] % nchunks == 0
    gate = lambda name: pl.BlockSpec((nchunks, tm, LANES),
                                     lambda i: (lay[name][0] // nchunks, i, 0))
    resident = lambda rows: pl.BlockSpec((None, rows, d), lambda i: (layer, 0, 0),
                                         pipeline_mode=pl.Buffered(1))
    return pl.pallas_call(
        functools.partial(_mix_out_kernel, hd=hd, tm=tm),
        out_shape=jax.ShapeDtypeStruct((m, d), _F32),
        grid=(m // tm,),
        in_specs=[
            pl.BlockSpec((tm, ka), lambda i: (i, 0)),
            pl.BlockSpec((hd, tm, HEAD_DIM), lambda i: (0, i, 0)),
            resident(ka), resident(hd * HEAD_DIM),
            gate("ga"), gate("gb"),
            resident(d),
            pl.BlockSpec((tm, d), lambda i: (i, 0)),
            pl.BlockSpec((None, 1, d), lambda i: (i // tps, 0, 0)),
        ],
        out_specs=pl.BlockSpec((tm, d), lambda i: (i, 0)),
        scratch_shapes=[pltpu.VMEM((tm, d), _BF16)],
        compiler_params=_cparams(1),
        name="mix_out",
    )(ya, yb, wa, wb, proj, proj, wo, x2, gt)


def _out_res_kernel(a_ref, w_ref, x_ref, gt_ref, o_ref):
    acc = jnp.dot(a_ref[...], w_ref[...], preferred_element_type=_F32)
    o_ref[...] = x_ref[...] + (1.0 + gt_ref[...]) * acc


def _out_res(a, w, layer, x2, gt, seq, tm, tn):
    m, k = a.shape
    d = w.shape[2]
    tps = seq // tm
    return pl.pallas_call(
        _out_res_kernel,
        out_shape=jax.ShapeDtypeStruct((m, d), _F32),
        grid=(m // tm, d // tn),
        in_specs=[
            pl.BlockSpec((tm, k), lambda i, j: (i, 0)),
            pl.BlockSpec((None, k, tn), lambda i, j: (layer, 0, j)),
            pl.BlockSpec((tm, tn), lambda i, j: (i, j)),
            pl.BlockSpec((None, 1, tn), lambda i, j: (i // tps, 0, j)),
        ],
        out_specs=pl.BlockSpec((tm, tn), lambda i, j: (i, j)),
        compiler_params=_cparams(2),
        name="out_res",
    )(a, w, x2, gt)


def _shift_rows(u, k, prev):
    rolled = pltpu.roll(u, k, axis=0)
    first = lax.broadcasted_iota(jnp.int32, prev.shape, 0) < k
    head = jnp.where(first, pltpu.roll(prev, k, axis=0), rolled[:SUBLANES])
    return jnp.concatenate([head, rolled[SUBLANES:]], axis=0)


def _ffn_up_kernel(x_ref, g_ref, sc_ref, sh_ref, wa_ref, wb_ref, cp_ref,
                   o_ref, h_ref, halo_ref, carry_ref, *, tm, tn, tps):
    i = pl.program_id(0)
    j = pl.program_id(1)
    rc = min(ROW_CHUNK, tm)

    @pl.when(j == 0)
    def _():
        h = _norm_modulate(x_ref[...], g_ref[...], sc_ref[...], sh_ref[...])
        h_ref[...] = h.astype(_BF16)

    @pl.when(i % tps == 0)
    def _():
        halo_ref[...] = jnp.zeros_like(halo_ref)

    @pl.when(i % tps != 0)
    def _():
        halo_ref[...] = carry_ref[j]

    cp = cp_ref[j]

    def conv(u, prev, cols):
        u1 = _shift_rows(u, 1, prev)
        u2 = _shift_rows(u, 2, prev)
        return cp[3:4, cols] + (cp[0:1, cols] * u2 + cp[1:2, cols] * u1 + cp[2:3, cols] * u)

    prev_a = halo_ref[:, 0:tn]
    prev_b = halo_ref[:, tn:2 * tn]
    for r in range(tm // rc):
        lo = r * rc
        hr = h_ref[lo:lo + rc, :]
        ua = jnp.dot(hr, wa_ref[...], preferred_element_type=_F32)
        ub = jnp.dot(hr, wb_ref[...], preferred_element_type=_F32)
        a = conv(ua, prev_a, slice(0, tn))
        b = conv(ub, prev_b, slice(tn, 2 * tn))
        o_ref[lo:lo + rc, :] = (a * jax.nn.sigmoid(a) * b).astype(_BF16)
        prev_a = ua[rc - SUBLANES:, :]
        prev_b = ub[rc - SUBLANES:, :]

    carry_ref[j] = jnp.concatenate([prev_a, prev_b], axis=1)


def _conv_params(conv_w, conv_b, tn):
    depth, taps, width = conv_w.shape
    nblk = width // (2 * tn)
    rows = jnp.concatenate([conv_w, conv_b[:, None, :],
                            jnp.zeros((depth, SUBLANES - taps - 1, width), conv_w.dtype)], axis=1)
    rows = rows.reshape(depth, SUBLANES, 2, nblk, tn)
    return jnp.transpose(rows, (0, 3, 1, 2, 4)).reshape(depth, nblk, SUBLANES, 2 * tn)


def _ffn_up(x2, g, sc, sh, w, cp, layer, seq, tm, tn):
    m, d = x2.shape
    nblk = w.shape[2] // (2 * tn)
    tps = seq // tm
    mod_spec = pl.BlockSpec((None, 1, d), lambda i, j: (i // tps, 0, 0))
    halves = lambda rows: (pl.BlockSpec((None, rows, tn), lambda i, j: (layer, 0, j)),
                           pl.BlockSpec((None, rows, tn), lambda i, j: (layer, 0, nblk + j)))
    return pl.pallas_call(
        functools.partial(_ffn_up_kernel, tm=tm, tn=tn, tps=tps),
        out_shape=jax.ShapeDtypeStruct((m, nblk * tn), _BF16),
        grid=(m // tm, nblk),
        in_specs=[
            pl.BlockSpec((tm, d), lambda i, j: (i, 0)),
            pl.BlockSpec((None, 1, d), lambda i, j: (layer, 0, 0)),
            mod_spec, mod_spec,
            *halves(d),
            pl.BlockSpec((None, nblk, SUBLANES, 2 * tn), lambda i, j: (layer, 0, 0, 0)),
        ],
        out_specs=pl.BlockSpec((tm, tn), lambda i, j: (i, j)),
        scratch_shapes=[pltpu.VMEM((tm, d), _BF16),
                        pltpu.VMEM((SUBLANES, 2 * tn), _F32),
                        pltpu.VMEM((nblk, SUBLANES, 2 * tn), _F32)],
        compiler_params=_cparams(2),
        name="ffn_up",
    )(x2, g, sc, sh, w, w, cp)


def _final_norm_kernel(x_ref, g_ref, o_ref):
    x = x_ref[...]
    ms = jnp.mean(x * x, axis=-1, keepdims=True)
    o_ref[...] = (x * lax.rsqrt(ms + NORM_EPS)) * g_ref[...]


def _final_norm(x2, g, tm):
    m, d = x2.shape
    return pl.pallas_call(
        _final_norm_kernel,
        out_shape=jax.ShapeDtypeStruct((m, d), _F32),
        grid=(m // tm,),
        in_specs=[pl.BlockSpec((tm, d), lambda i: (i, 0)),
                  pl.BlockSpec((1, d), lambda i: (0, 0))],
        out_specs=pl.BlockSpec((tm, d), lambda i: (i, 0)),
        compiler_params=_cparams(1),
        name="final_norm",
    )(x2, g)


def _arrange_w_in(w_in, d, hs, hd, hi, misc_chunks):
    sw, dw, iw = hs * HEAD_DIM, hd * HEAD_DIM, hi * IDX_DIM
    sizes = (sw, sw, sw, dw, dw, dw, iw, IDX_DIM, hi, d, d)
    offs = [0]
    for s in sizes:
        offs.append(offs[-1] + s)
    qa, ka, va, qb, kb, vb, qi, ki, wi, ga, gb = (
        w_in[..., offs[n]:offs[n + 1]] for n in range(len(sizes)))
    z = lambda n: jnp.zeros(w_in.shape[:-1] + (n,), w_in.dtype)
    misc = jnp.concatenate([ki, z(LANES - IDX_DIM), z(LANES - IDX_DIM), ki,
                            wi, z(LANES - hi), z((misc_chunks - 3) * LANES)], axis=-1)
    return jnp.concatenate([ga, gb, qa, ka, va, qb, kb, vb, qi, misc], axis=-1).astype(_BF16)


def _rope_tables(positions):
    pos = positions.reshape(-1).astype(_F32)

    def cos_sin(dim):
        inv_freq = 1.0 / (ROPE_THETA ** (jnp.arange(0, dim, 2, dtype=_F32) / dim))
        ang = pos[:, None] * inv_freq
        return jnp.cos(ang), jnp.sin(ang)

    c, s = cos_sin(HEAD_DIM)
    c128 = jnp.concatenate([c, c], axis=-1)
    s128 = jnp.concatenate([-s, s], axis=-1)
    c, s = cos_sin(IDX_DIM)
    zero = jnp.zeros_like(s)
    c64 = jnp.concatenate([c, c, c, c], axis=-1)
    s64lo = jnp.concatenate([-s, zero, -s, zero], axis=-1)
    s64hi = jnp.concatenate([zero, s, zero, s], axis=-1)
    return c128, s128, c64, s64lo, s64hi


def _pick(n, prefs):
    for p in prefs:
        if n % p == 0:
            return p
    return n


def kernel(x, c, positions, w_in, w_a, w_b, w_o, w_ada, b_ada, g_mix, g_ffn, w_up, conv_w,
           conv_b, w_down, g_final):
    batch, seq, d = x.shape
    depth = w_in.shape[0]
    hs = w_a.shape[1] // HEAD_DIM
    hd = w_b.shape[1] // HEAD_DIM
    rest = w_in.shape[2] - 3 * hs * HEAD_DIM - 3 * hd * HEAD_DIM - 2 * d - IDX_DIM
    hi = rest // (IDX_DIM + 1)
    dff = w_down.shape[1]
    m = batch * seq
    lay = _proj_layout(d, hs, hd, hi)

    tm = _pick(seq, (1024, 512, 256, 128))
    tm_mix = _pick(seq, (512, 256, 128))
    t_sb = _pick(seq, (256, 128))
    nh_sb = 2
    tq_dsa = _pick(seq, (256, 128))
    tn_ffn = _pick(dff, (512, 256, 128))
    tn_out = _pick(d, (512, 256, 128))

    w_in_r = _arrange_w_in(w_in, d, hs, hd, hi, lay["misc"][1])
    w_a16, w_b16, w_o16, w_up16, w_down16 = (t.astype(_BF16)
                                             for t in (w_a, w_b, w_o, w_up, w_down))
    tabs = _rope_tables(positions)
    mod = _modulation(c, w_ada, b_ada).reshape(depth, batch, N_MOD, 1, d)
    g_mix3, g_ffn3 = g_mix.reshape(depth, 1, d), g_ffn.reshape(depth, 1, d)
    conv_p = _conv_params(conv_w, conv_b, tn_ffn)

    x2 = x.reshape(m, d)
    for l in range(depth):
        sh1, sc1, gt1, sh2, sc2, gt2 = (mod[l, :, n] for n in range(N_MOD))
        proj, wi = _in_proj(x2, g_mix3, sc1, sh1, w_in_r, l, tabs, lay, hi, seq, tm)
        ya = _sb_attention(proj, lay, batch, seq, hs, t_sb, nh_sb)
        yb = _dsa_attention(proj, wi, lay, batch, seq, hi, hd, tq_dsa)
        x2 = _mix_out(ya, yb, w_a16, w_b16, w_o16, l, proj, x2, gt1, lay, seq, tm_mix)
        g = _ffn_up(x2, g_ffn3, sc2, sh2, w_up16, conv_p, l, seq, tm, tn_ffn)
        x2 = _out_res(g, w_down16, l, x2, gt2, seq, tm, tn_out)
    out = _final_norm(x2, g_final.reshape(1, d), tm)
    return out.reshape(batch, seq, d)
```

```python
import functools
import math

import jax
import jax.numpy as jnp
from jax import lax
from jax.experimental import pallas as pl
from jax.experimental.pallas import tpu as pltpu

HEAD_DIM = 128
IDX_DIM = 64
TOPK_MAX = 256
CONV_WIDTH = 3
ROPE_THETA = 10000.0
NORM_EPS = 1e-6
N_MOD = 6
LANES = 128
SUBLANES = 8
CHUNKS_PER_BLOCK = 4
PROJ_TN = LANES * CHUNKS_PER_BLOCK
PROJ_BLOCKS_PER_STEP = 1
FFN_BLOCKS_PER_STEP = 2
VMEM_LIMIT = 56 * 1024 * 1024
INT_MIN = -2 ** 31
I16_MIN, I16_MAX = -2 ** 15, 2 ** 15 - 1
PACK16 = 16
NEG_BIG = -1e30
ROW_CHUNK = 256
LOG2E = math.log2(math.e)

_F32 = jnp.float32
_BF16 = jnp.bfloat16
_NT = (((1,), (1,)), ((), ()))


def _cparams(n_axes):
    return pltpu.CompilerParams(dimension_semantics=("arbitrary",) * n_axes,
                                vmem_limit_bytes=VMEM_LIMIT)


def _mod_kernel(c_ref, w_ref, b_ref, o_ref):
    c = c_ref[...]
    c_act = (c * jax.nn.sigmoid(c)).astype(_BF16)
    acc = jnp.dot(c_act, w_ref[...].astype(_BF16), preferred_element_type=_F32)
    o_ref[...] = acc + b_ref[...]


def _modulation(c, w_ada, b_ada):
    depth, d, n = w_ada.shape
    b = c.shape[0]
    tn = 1024 if n % 1024 == 0 else n
    return pl.pallas_call(
        _mod_kernel,
        out_shape=jax.ShapeDtypeStruct((depth, b, n), _F32),
        grid=(depth, n // tn),
        in_specs=[
            pl.BlockSpec((b, d), lambda l, j: (0, 0)),
            pl.BlockSpec((None, d, tn), lambda l, j: (l, 0, j)),
            pl.BlockSpec((None, 1, tn), lambda l, j: (l, 0, j)),
        ],
        out_specs=pl.BlockSpec((None, b, tn), lambda l, j: (l, 0, j)),
        compiler_params=_cparams(2),
        name="adaln_mod",
    )(c, w_ada, b_ada.reshape(depth, 1, n))


def _norm_modulate(x, g, sc, sh):
    ms = jnp.mean(x * x, axis=-1, keepdims=True)
    r = x * lax.rsqrt(ms + NORM_EPS)
    return (r * g) * (1.0 + sc) + sh


def _proj_layout(d, hs, hd, hi):
    sizes = [("ga", d // LANES), ("gb", d // LANES),
             ("qa", hs), ("ka", hs), ("va", hs),
             ("qb", hd), ("kb", hd), ("vb", hd),
             ("qi", hi * IDX_DIM // LANES)]
    spb = PROJ_BLOCKS_PER_STEP
    sizes.append(("misc", spb * CHUNKS_PER_BLOCK))
    off, lay = 0, {"spb": spb}
    for name, n in sizes:
        assert n % (spb * CHUNKS_PER_BLOCK) == 0, (name, n)
        lay[name] = (off, n)
        off += n
    lay["total"] = off
    return lay


def _rope128(t, cos, sin):
    return t * cos + pltpu.roll(t, HEAD_DIM // 2, axis=1) * sin


def _rope64(t, cos, sin_lo, sin_hi):
    half = IDX_DIM // 2
    return (t * cos + pltpu.roll(t, LANES - half, axis=1) * sin_lo
            + pltpu.roll(t, half, axis=1) * sin_hi)


def _in_proj_kernel(x_ref, g_ref, sc_ref, sh_ref, w_ref, c128_ref, s128_ref,
                    c64_ref, s64lo_ref, s64hi_ref, o_ref, wi_ref, h_ref, *, lay, idx_scale, tm):
    j = pl.program_id(1)

    @pl.when(j == 0)
    def _():
        h = _norm_modulate(x_ref[...], g_ref[...], sc_ref[...], sh_ref[...])
        h_ref[...] = h.astype(_BF16)

    spb = lay["spb"]
    step_chunks = spb * CHUNKS_PER_BLOCK
    blk = lambda name: (lay[name][0] // step_chunks, (lay[name][0] + lay[name][1]) // step_chunks)
    qa_scale = (HEAD_DIM ** -0.5) * LOG2E
    qb_scale = (HEAD_DIM ** -0.5) * LOG2E
    rc = min(ROW_CHUNK, tm)

    def run(epilogue, nsub=spb):
        for sub in range(nsub):
            for r in range(tm // rc):
                rows = slice(r * rc, (r + 1) * rc)
                acc = jnp.dot(h_ref[rows, :], w_ref[:, sub * PROJ_TN:(sub + 1) * PROJ_TN],
                              preferred_element_type=_F32)
                epilogue(acc, rows, sub * CHUNKS_PER_BLOCK)

    def per_chunk(fn):
        def epilogue(acc, rows, c0):
            for c in range(CHUNKS_PER_BLOCK):
                o_ref[c0 + c, rows, :] = fn(acc[:, c * LANES:(c + 1) * LANES], rows).astype(_BF16)
        return epilogue

    def in_group(*names):
        cond = None
        for nm in names:
            lo, hi = blk(nm)
            t = jnp.logical_and(j >= lo, j < hi)
            cond = t if cond is None else jnp.logical_or(cond, t)
        return cond

    rope128 = lambda t, rows: _rope128(t, c128_ref[rows, :], s128_ref[rows, :])
    rope64 = lambda t, rows: _rope64(t, c64_ref[rows, :], s64lo_ref[rows, :], s64hi_ref[rows, :])

    @pl.when(in_group("ga", "gb"))
    def _():
        run(per_chunk(lambda t, rows: jax.nn.sigmoid(t)))

    @pl.when(in_group("qa"))
    def _():
        run(per_chunk(lambda t, rows: t * qa_scale))

    @pl.when(in_group("ka", "va", "vb"))
    def _():
        run(per_chunk(lambda t, rows: t))

    @pl.when(in_group("qb"))
    def _():
        run(per_chunk(lambda t, rows: rope128(t, rows) * qb_scale))

    @pl.when(in_group("kb"))
    def _():
        run(per_chunk(rope128))

    @pl.when(in_group("qi"))
    def _():
        run(per_chunk(rope64))

    @pl.when(in_group("misc"))
    def _():
        def epilogue(acc, rows, c0):
            for c in range(2):
                o_ref[c, rows, :] = rope64(acc[:, c * LANES:(c + 1) * LANES], rows).astype(_BF16)
            o_ref[2, rows, :] = acc[:, 2 * LANES:3 * LANES].astype(_BF16)
            wi_ref[rows, :] = acc[:, 2 * LANES:3 * LANES] * idx_scale
        run(epilogue, nsub=1)
        o_ref[3:step_chunks] = jnp.zeros((step_chunks - 3, tm, LANES), _BF16)


def _in_proj(x2, g, sc, sh, w, layer, tabs, lay, hi, seq, tm):
    m, d = x2.shape
    step_chunks = lay["spb"] * CHUNKS_PER_BLOCK
    nblk = lay["total"] // step_chunks
    tps = seq // tm
    row_spec = pl.BlockSpec((tm, LANES), lambda i, j: (i, 0))
    mod_spec = pl.BlockSpec((None, 1, d), lambda i, j: (i // tps, 0, 0))
    kern = functools.partial(_in_proj_kernel, lay=lay, tm=tm,
                             idx_scale=(IDX_DIM ** -0.5) * (hi ** -0.5))
    return pl.pallas_call(
        kern,
        out_shape=(jax.ShapeDtypeStruct((lay["total"], m, LANES), _BF16),
                   jax.ShapeDtypeStruct((m, LANES), _F32)),
        grid=(m // tm, nblk),
        in_specs=[
            pl.BlockSpec((tm, d), lambda i, j: (i, 0)),
            pl.BlockSpec((None, 1, d), lambda i, j: (layer, 0, 0)),
            mod_spec, mod_spec,
            pl.BlockSpec((None, d, step_chunks * LANES), lambda i, j: (layer, 0, j)),
            row_spec, row_spec, row_spec, row_spec, row_spec,
        ],
        out_specs=(pl.BlockSpec((step_chunks, tm, LANES), lambda i, j: (j, i, 0)),
                   pl.BlockSpec((tm, LANES), lambda i, j: (i, 0))),
        scratch_shapes=[pltpu.VMEM((tm, d), _BF16)],
        compiler_params=_cparams(2),
        name="in_proj",
    )(x2, g, sc, sh, w, *tabs)


def _add_lane_replicated(x, rep):
    n = x.shape[1] // LANES
    return jnp.concatenate([x[:, c * LANES:(c + 1) * LANES] + rep for c in range(n)], axis=1)


def _sb_kernel(q_ref, k_ref, v_ref, tri_ref, o_ref, *, t, nh, nq, qps):
    qi = pl.program_id(2)

    def variant(n, rows):
        past = (lax.broadcasted_iota(jnp.int32, (t, t), 1)
                < lax.broadcasted_iota(jnp.int32, (t, t), 0))

        def front(hh, kb):
            k = k_ref[hh, kb * t:(kb + 1) * t, :]
            z = lax.dot_general(q_ref[hh, rows, :], k, _NT,
                                preferred_element_type=_F32)
            sp = jnp.maximum(z, 0.0) + jnp.log(1.0 + jnp.exp2(-jnp.abs(z))) * LOG2E
            spm = (jnp.where(past, sp, 0.0) if kb == n else sp).astype(_BF16)
            cum = jnp.dot(spm, tri_ref[...], preferred_element_type=_F32)
            total = cum[:, 0:1] - spm[:, 0:1].astype(_F32)
            return z - sp, cum, total

        def back(hh, kb, zs, cum, carry):
            w = jnp.exp2(zs + _add_lane_replicated(cum, carry))
            if kb == n:
                w = jnp.where(past, w, 0.0)
            v = v_ref[hh, kb * t:(kb + 1) * t, :]
            return jnp.dot(w.astype(_BF16), v, preferred_element_type=_F32)

        items = [(hh, kb) for hh in range(nh) for kb in range(n, -1, -1)]
        ahead = front(*items[0])
        for idx, (hh, kb) in enumerate(items):
            zs, cum, total = ahead
            if idx + 1 < len(items):
                ahead = front(*items[idx + 1])
            if kb == n:
                acc = jnp.zeros((t, HEAD_DIM), _F32)
                carry = jnp.zeros((t, LANES), _F32)
            acc = acc + back(hh, kb, zs, cum, carry)
            carry = carry + jnp.broadcast_to(total, (t, LANES))
            if kb == 0:
                o_ref[rows, hh * HEAD_DIM:(hh + 1) * HEAD_DIM] = acc.astype(_BF16)

    def step(m):
        for sub in range(qps):
            variant(m * qps + sub, slice(sub * t, (sub + 1) * t))

    for m in range(nq // qps):
        pl.when(qi == m)(functools.partial(step, m))


def _sb_attention(proj, lay, batch, seq, hs, t, nh):
    m = batch * seq
    nq = seq // t
    qps = _pick(nq, (8, 4, 2, 1))
    ns = nq // qps
    qa, ka, va = (lay[n][0] // nh for n in ("qa", "ka", "va"))
    assert hs % nh == 0 and all(lay[n][0] % nh == 0 for n in ("qa", "ka", "va"))
    r = lax.broadcasted_iota(jnp.int32, (t, t), 0)
    c = lax.broadcasted_iota(jnp.int32, (t, t), 1)
    tri = jnp.where(r > c, -1.0, 0.0).astype(_BF16)
    return pl.pallas_call(
        functools.partial(_sb_kernel, t=t, nh=nh, nq=nq, qps=qps),
        out_shape=jax.ShapeDtypeStruct((m, hs * HEAD_DIM), _BF16),
        grid=(batch, hs // nh, ns),
        in_specs=[
            pl.BlockSpec((nh, qps * t, HEAD_DIM), lambda b, h, i: (qa + h, b * ns + i, 0)),
            pl.BlockSpec((nh, seq, HEAD_DIM), lambda b, h, i: (ka + h, b, 0)),
            pl.BlockSpec((nh, seq, HEAD_DIM), lambda b, h, i: (va + h, b, 0)),
            pl.BlockSpec((t, t), lambda b, h, i: (0, 0)),
        ],
        out_specs=pl.BlockSpec((qps * t, nh * HEAD_DIM), lambda b, h, i: (b * ns + i, h)),
        compiler_params=_cparams(3),
        name="sb_attn",
    )(proj, proj, proj, tri)


def _sortable(score):
    u = lax.bitcast_convert_type(score, jnp.int32)
    return u ^ (lax.shift_right_arithmetic(u, 31) & 0x7FFFFFFF)


def _dsa_kernel(qi_ref, ke_ref, wi_ref, qb_ref, kb_ref, vb_ref, lowtri_ref, o_ref,
                wt_ref, vt_ref, score_ref, key_ref, k16_ref, bias_ref, *, tq, seq, hi, hd, topk,
                nbuckets):
    i = pl.program_id(1)
    npairs = hi // 2
    tchunk = 256 if seq % 256 == 0 else LANES

    @pl.when(i == 0)
    def _():
        def xpose(h, c):
            for r in range(seq // tchunk):
                blk = vb_ref[h, r * tchunk:(r + 1) * tchunk, :].astype(_F32)
                vt_ref[h, :, r * tchunk:(r + 1) * tchunk] = blk.T.astype(_BF16)
            return c
        lax.fori_loop(0, hd, xpose, 0)

    wt_ref[...] = wi_ref[...].T

    def run(width):
        key_pos = lax.broadcasted_iota(jnp.int32, (width, tq), 0)
        q_pos = i * tq + lax.broadcasted_iota(jnp.int32, (width, tq), 1)
        visible = key_pos <= q_pos

        def pair(p):
            qp = qi_ref[p]
            d_even = lax.dot_general(ke_ref[0, :width, :], qp, _NT, preferred_element_type=_F32)
            d_odd = lax.dot_general(ke_ref[1, :width, :], qp, _NT, preferred_element_type=_F32)
            return (wt_ref[pl.ds(2 * p, 1), :] * jnp.maximum(d_even, 0.0)
                    + wt_ref[pl.ds(2 * p + 1, 1), :] * jnp.maximum(d_odd, 0.0))

        score_ref[:width, :] = jnp.zeros((width, tq), _F32)
        group = _pick(npairs, (4, 2, 1))

        def pairs(g, c):
            for u in range(group):
                score_ref[:width, :] += pair(g * group + u)
            return c

        lax.fori_loop(0, npairs // group, pairs, 0)
        key_ref[:width, :] = jnp.where(visible, _sortable(score_ref[:width, :]), INT_MIN)

        def count_ge16(cand):
            hit = jnp.where(k16_ref[:width, :] >= cand.astype(jnp.int16), jnp.int16(1), jnp.int16(0))
            parts = [hit[r * PACK16:(r + 1) * PACK16, :] for r in range(width // PACK16)]
            while len(parts) > 1:
                odd = parts[-1:] if len(parts) % 2 else []
                parts = [a + b for a, b in zip(parts[0::2], parts[1::2])] + odd
            return jnp.sum(parts[0].astype(jnp.int32), axis=0, keepdims=True)

        def search16(need):
            prefix = jnp.where(count_ge16(jnp.zeros((1, tq), jnp.int32)) >= need, 0, I16_MIN)

            def step(b, prefix):
                cand = prefix + lax.shift_left(jnp.int32(1), 14 - b)
                return jnp.where(count_ge16(cand) >= need, cand, prefix)

            return lax.fori_loop(0, 15, step, prefix.astype(jnp.int32))

        key_hi = lax.shift_right_arithmetic(key_ref[:width, :], 16)
        k16_ref[:width, :] = key_hi.astype(jnp.int16)
        t_hi = search16(jnp.full((1, tq), topk, jnp.int32))
        n_above = jnp.where(t_hi < I16_MAX, count_ge16(jnp.minimum(t_hi + 1, I16_MAX)), 0)
        key = key_ref[:width, :]
        key_lo = (key & 0xFFFF) + I16_MIN
        tied = lax.shift_right_arithmetic(key, 16) == t_hi
        k16_ref[:width, :] = jnp.where(tied, key_lo, I16_MIN).astype(jnp.int16)
        t_lo = search16(topk - n_above)
        thr = t_hi * 65536 + (t_lo - I16_MIN)
        thr = jnp.maximum(thr, INT_MIN + 1)
        selected = key_ref[:width, :] >= thr
        bias_ref[:width, :] = jnp.where(selected, 0.0, NEG_BIG)
        n_selected = jnp.sum(jnp.where(selected, 1.0, 0.0), axis=0, keepdims=True)

        @pl.when(jnp.max(n_selected) > topk)
        def _():
            key = key_ref[:width, :]
            above = key > thr
            need = topk - jnp.sum(jnp.where(above, 1.0, 0.0), axis=0, keepdims=True)

            def block(r, before):
                rows = pl.ds(pl.multiple_of(r * tq, tq), tq)
                kblk = key_ref[rows, :]
                tie = kblk == thr
                within = jnp.dot(lowtri_ref[...], jnp.where(tie, 1.0, 0.0).astype(_BF16),
                                 preferred_element_type=_F32)
                keep = jnp.logical_and(tie, within + before <= need)
                bias_ref[rows, :] = jnp.where(jnp.logical_or(kblk > thr, keep), 0.0, NEG_BIG)
                return before + within[tq - 1:tq, :]

            lax.fori_loop(0, width // tq, block, jnp.zeros((1, tq), _F32))

        def logits(h):
            s = lax.dot_general(kb_ref[h, :width, :], qb_ref[h], _NT, preferred_element_type=_F32)
            return s + bias_ref[:width, :]

        def finish(h, s):
            mx = jnp.max(s, axis=0, keepdims=True)
            p = jnp.exp2(s - mx)
            den = jnp.sum(p, axis=0, keepdims=True)
            ot = jnp.dot(vt_ref[h, :, :width], p.astype(_BF16), preferred_element_type=_F32)
            o_ref[h] = (ot / den).T.astype(_BF16)

        group_h = _pick(hd, (4, 2, 1))

        def heads(g, c):
            h0 = g * group_h
            s_next = logits(h0)
            for u in range(group_h):
                s_cur = s_next
                if u + 1 < group_h:
                    s_next = logits(h0 + u + 1)
                finish(h0 + u, s_cur)
            return c

        lax.fori_loop(0, hd // group_h, heads, 0)

    nq = seq // tq
    per = nq // nbuckets
    bucket = i // per
    for bi in range(nbuckets):
        pl.when(bucket == bi)(functools.partial(run, (bi + 1) * per * tq))


def _dsa_attention(proj, wi, lay, batch, seq, hi, hd, tq):
    m = batch * seq
    nq = seq // tq
    topk = min(TOPK_MAX, seq // 4)
    nbuckets = 4 if nq % 4 == 0 else 1
    npairs = hi // 2
    qi_b = lay["qi"][0] // npairs
    ke_b = lay["misc"][0] // 2
    qb_b, kb_b, vb_b = (lay[n][0] // hd for n in ("qb", "kb", "vb"))
    assert lay["qi"][0] % npairs == 0 and lay["misc"][0] % 2 == 0
    assert all(lay[n][0] % hd == 0 for n in ("qb", "kb", "vb"))
    kern = functools.partial(_dsa_kernel, tq=tq, seq=seq, hi=hi, hd=hd, topk=topk,
                             nbuckets=nbuckets)
    r = lax.broadcasted_iota(jnp.int32, (tq, tq), 0)
    c = lax.broadcasted_iota(jnp.int32, (tq, tq), 1)
    lowtri = jnp.where(c <= r, 1.0, 0.0).astype(_BF16)
    return pl.pallas_call(
        kern,
        out_shape=jax.ShapeDtypeStruct((hd, m, HEAD_DIM), _BF16),
        grid=(batch, nq),
        in_specs=[
            pl.BlockSpec((npairs, tq, LANES), lambda b, i: (qi_b, b * nq + i, 0)),
            pl.BlockSpec((2, seq, LANES), lambda b, i: (ke_b, b, 0)),
            pl.BlockSpec((tq, LANES), lambda b, i: (b * nq + i, 0)),
            pl.BlockSpec((hd, tq, HEAD_DIM), lambda b, i: (qb_b, b * nq + i, 0)),
            pl.BlockSpec((hd, seq, HEAD_DIM), lambda b, i: (kb_b, b, 0)),
            pl.BlockSpec((hd, seq, HEAD_DIM), lambda b, i: (vb_b, b, 0)),
            pl.BlockSpec((tq, tq), lambda b, i: (0, 0)),
        ],
        out_specs=pl.BlockSpec((hd, tq, HEAD_DIM), lambda b, i: (0, b * nq + i, 0)),
        scratch_shapes=[pltpu.VMEM((LANES, tq), _F32),
                        pltpu.VMEM((hd, HEAD_DIM, seq), _BF16),
                        pltpu.VMEM((seq, tq), _F32),
                        pltpu.VMEM((seq, tq), jnp.int32),
                        pltpu.VMEM((seq, tq), jnp.int16),
                        pltpu.VMEM((seq, tq), _F32)],
        compiler_params=_cparams(2),
        name="dsa_attn",
    )(proj, proj, wi, proj, proj, proj, lowtri)


def _mix_out_kernel(ya_ref, yb_ref, wa_ref, wb_ref, ga_ref, gb_ref, wo_ref, x_ref, gt_ref,
                    o_ref, mg_ref, *, hd, tm):
    rc = min(ROW_CHUNK, tm)
    for r in range(tm // rc):
        rows = slice(r * rc, (r + 1) * rc)
        yb = jnp.concatenate([yb_ref[h, rows, :] for h in range(hd)], axis=1)
        a = jnp.dot(ya_ref[rows, :], wa_ref[...], preferred_element_type=_F32)
        b = jnp.dot(yb, wb_ref[...], preferred_element_type=_F32)
        for c in range(a.shape[1] // LANES):
            sl = slice(c * LANES, (c + 1) * LANES)
            mg_ref[rows, sl] = (ga_ref[c, rows, :].astype(_F32) * a[:, sl]
                                + gb_ref[c, rows, :].astype(_F32) * b[:, sl]).astype(_BF16)
    for r in range(tm // rc):
        rows = slice(r * rc, (r + 1) * rc)
        acc = jnp.dot(mg_ref[rows, :], wo_ref[...], preferred_element_type=_F32)
        o_ref[rows, :] = x_ref[rows, :] + (1.0 + gt_ref[...]) * acc


def _mix_out(ya, yb, wa, wb, wo, layer, proj, x2, gt, lay, seq, tm):
    m, ka = ya.shape
    hd = yb.shape[0]
    d = wo.shape[2]
    nchunks = d // LANES
    tps = seq // tm
    assert lay["ga"][0] % nchunks == 0 and lay["gb"][0] % nchunks == 0
    gate = lambda name: pl.BlockSpec((nchunks, tm, LANES),
                                     lambda i: (lay[name][0] // nchunks, i, 0))
    resident = lambda rows: pl.BlockSpec((None, rows, d), lambda i: (layer, 0, 0),
                                         pipeline_mode=pl.Buffered(1))
    return pl.pallas_call(
        functools.partial(_mix_out_kernel, hd=hd, tm=tm),
        out_shape=jax.ShapeDtypeStruct((m, d), _F32),
        grid=(m // tm,),
        in_specs=[
            pl.BlockSpec((tm, ka), lambda i: (i, 0)),
            pl.BlockSpec((hd, tm, HEAD_DIM), lambda i: (0, i, 0)),
            resident(ka), resident(hd * HEAD_DIM),
            gate("ga"), gate("gb"),
            resident(d),
            pl.BlockSpec((tm, d), lambda i: (i, 0)),
            pl.BlockSpec((None, 1, d), lambda i: (i // tps, 0, 0)),
        ],
        out_specs=pl.BlockSpec((tm, d), lambda i: (i, 0)),
        scratch_shapes=[pltpu.VMEM((tm, d), _BF16)],
        compiler_params=_cparams(1),
        name="mix_out",
    )(ya, yb, wa, wb, proj, proj, wo, x2, gt)


def _out_res_kernel(a_ref, w_ref, x_ref, gt_ref, o_ref):
    acc = jnp.dot(a_ref[...], w_ref[...], preferred_element_type=_F32)
    o_ref[...] = x_ref[...] + (1.0 + gt_ref[...]) * acc


def _out_res(a, w, layer, x2, gt, seq, tm, tn):
    m, k = a.shape
    d = w.shape[2]
    tps = seq // tm
    return pl.pallas_call(
        _out_res_kernel,
        out_shape=jax.ShapeDtypeStruct((m, d), _F32),
        grid=(m // tm, d // tn),
        in_specs=[
            pl.BlockSpec((tm, k), lambda i, j: (i, 0)),
            pl.BlockSpec((None, k, tn), lambda i, j: (layer, 0, j)),
            pl.BlockSpec((tm, tn), lambda i, j: (i, j)),
            pl.BlockSpec((None, 1, tn), lambda i, j: (i // tps, 0, j)),
        ],
        out_specs=pl.BlockSpec((tm, tn), lambda i, j: (i, j)),
        compiler_params=_cparams(2),
        name="out_res",
    )(a, w, x2, gt)


def _shift_rows(u, k, prev):
    rolled = pltpu.roll(u, k, axis=0)
    first = lax.broadcasted_iota(jnp.int32, prev.shape, 0) < k
    head = jnp.where(first, pltpu.roll(prev, k, axis=0), rolled[:SUBLANES])
    return jnp.concatenate([head, rolled[SUBLANES:]], axis=0)


def _ffn_up_kernel(x_ref, g_ref, sc_ref, sh_ref, *refs, tm, tn, tps, cps, nblk):
    w_refs, (cp_ref, o_ref, h_ref, halo_ref, carry_ref) = refs[:2 * cps], refs[2 * cps:]
    i = pl.program_id(0)
    j = pl.program_id(1)
    rc = min(ROW_CHUNK, tm)

    @pl.when(j == 0)
    def _():
        h = _norm_modulate(x_ref[...], g_ref[...], sc_ref[...], sh_ref[...])
        h_ref[...] = h.astype(_BF16)

    def column_block(sub):
        jj = j * cps + sub
        wa_ref, wb_ref = w_refs[2 * sub], w_refs[2 * sub + 1]

        @pl.when(i % tps == 0)
        def _():
            halo_ref[...] = jnp.zeros_like(halo_ref)

        @pl.when(i % tps != 0)
        def _():
            halo_ref[...] = carry_ref[jj]

        cp = cp_ref[jj]

        def conv(u, prev, cols):
            u1 = _shift_rows(u, 1, prev)
            u2 = _shift_rows(u, 2, prev)
            return cp[3:4, cols] + (cp[0:1, cols] * u2 + cp[1:2, cols] * u1 + cp[2:3, cols] * u)

        prev_a = halo_ref[:, 0:tn]
        prev_b = halo_ref[:, tn:2 * tn]
        for r in range(tm // rc):
            lo = r * rc
            hr = h_ref[lo:lo + rc, :]
            ua = jnp.dot(hr, wa_ref[...], preferred_element_type=_F32)
            ub = jnp.dot(hr, wb_ref[...], preferred_element_type=_F32)
            a = conv(ua, prev_a, slice(0, tn))
            b = conv(ub, prev_b, slice(tn, 2 * tn))
            o_ref[lo:lo + rc, sub * tn:(sub + 1) * tn] = (a * jax.nn.sigmoid(a) * b).astype(_BF16)
            prev_a = ua[rc - SUBLANES:, :]
            prev_b = ub[rc - SUBLANES:, :]

        carry_ref[jj] = jnp.concatenate([prev_a, prev_b], axis=1)

    for sub in range(cps):
        if sub == 0 or nblk % cps == 0:
            column_block(sub)
        else:
            pl.when(j * cps + sub < nblk)(functools.partial(column_block, sub))


def _conv_params(conv_w, conv_b, tn):
    depth, taps, width = conv_w.shape
    nblk = width // (2 * tn)
    rows = jnp.concatenate([conv_w, conv_b[:, None, :],
                            jnp.zeros((depth, SUBLANES - taps - 1, width), conv_w.dtype)], axis=1)
    rows = rows.reshape(depth, SUBLANES, 2, nblk, tn)
    return jnp.transpose(rows, (0, 3, 1, 2, 4)).reshape(depth, nblk, SUBLANES, 2 * tn)


def _ffn_up(x2, g, sc, sh, w, cp, layer, seq, tm, tn):
    m, d = x2.shape
    nblk = w.shape[2] // (2 * tn)
    cps = FFN_BLOCKS_PER_STEP
    tps = seq // tm
    mod_spec = pl.BlockSpec((None, 1, d), lambda i, j: (i // tps, 0, 0))

    def halves(sub):
        col = lambda j: jnp.minimum(j * cps + sub, nblk - 1)
        return (pl.BlockSpec((None, d, tn), lambda i, j: (layer, 0, col(j))),
                pl.BlockSpec((None, d, tn), lambda i, j: (layer, 0, nblk + col(j))))

    w_specs = [s for sub in range(cps) for s in halves(sub)]
    return pl.pallas_call(
        functools.partial(_ffn_up_kernel, tm=tm, tn=tn, tps=tps, cps=cps, nblk=nblk),
        out_shape=jax.ShapeDtypeStruct((m, nblk * tn), _BF16),
        grid=(m // tm, pl.cdiv(nblk, cps)),
        in_specs=[
            pl.BlockSpec((tm, d), lambda i, j: (i, 0)),
            pl.BlockSpec((None, 1, d), lambda i, j: (layer, 0, 0)),
            mod_spec, mod_spec,
            *w_specs,
            pl.BlockSpec((None, nblk, SUBLANES, 2 * tn), lambda i, j: (layer, 0, 0, 0)),
        ],
        out_specs=pl.BlockSpec((tm, cps * tn), lambda i, j: (i, j)),
        scratch_shapes=[pltpu.VMEM((tm, d), _BF16),
                        pltpu.VMEM((SUBLANES, 2 * tn), _F32),
                        pltpu.VMEM((nblk, SUBLANES, 2 * tn), _F32)],
        compiler_params=_cparams(2),
        name="ffn_up",
    )(x2, g, sc, sh, *([w] * (2 * cps)), cp)


def _final_norm_kernel(x_ref, g_ref, o_ref):
    x = x_ref[...]
    ms = jnp.mean(x * x, axis=-1, keepdims=True)
    o_ref[...] = (x * lax.rsqrt(ms + NORM_EPS)) * g_ref[...]


def _final_norm(x2, g, tm):
    m, d = x2.shape
    return pl.pallas_call(
        _final_norm_kernel,
        out_shape=jax.ShapeDtypeStruct((m, d), _F32),
        grid=(m // tm,),
        in_specs=[pl.BlockSpec((tm, d), lambda i: (i, 0)),
                  pl.BlockSpec((1, d), lambda i: (0, 0))],
        out_specs=pl.BlockSpec((tm, d), lambda i: (i, 0)),
        compiler_params=_cparams(1),
        name="final_norm",
    )(x2, g)


def _arrange_w_in(w_in, d, hs, hd, hi, misc_chunks):
    sw, dw, iw = hs * HEAD_DIM, hd * HEAD_DIM, hi * IDX_DIM
    sizes = (sw, sw, sw, dw, dw, dw, iw, IDX_DIM, hi, d, d)
    offs = [0]
    for s in sizes:
        offs.append(offs[-1] + s)
    qa, ka, va, qb, kb, vb, qi, ki, wi, ga, gb = (
        w_in[..., offs[n]:offs[n + 1]] for n in range(len(sizes)))
    z = lambda n: jnp.zeros(w_in.shape[:-1] + (n,), w_in.dtype)
    misc = jnp.concatenate([ki, z(LANES - IDX_DIM), z(LANES - IDX_DIM), ki,
                            wi, z(LANES - hi), z((misc_chunks - 3) * LANES)], axis=-1)
    return jnp.concatenate([ga, gb, qa, ka, va, qb, kb, vb, qi, misc], axis=-1).astype(_BF16)


def _rope_tables(positions):
    pos = positions.reshape(-1).astype(_F32)

    def cos_sin(dim):
        inv_freq = 1.0 / (ROPE_THETA ** (jnp.arange(0, dim, 2, dtype=_F32) / dim))
        ang = pos[:, None] * inv_freq
        return jnp.cos(ang), jnp.sin(ang)

    c, s = cos_sin(HEAD_DIM)
    c128 = jnp.concatenate([c, c], axis=-1)
    s128 = jnp.concatenate([-s, s], axis=-1)
    c, s = cos_sin(IDX_DIM)
    zero = jnp.zeros_like(s)
    c64 = jnp.concatenate([c, c, c, c], axis=-1)
    s64lo = jnp.concatenate([-s, zero, -s, zero], axis=-1)
    s64hi = jnp.concatenate([zero, s, zero, s], axis=-1)
    return c128, s128, c64, s64lo, s64hi


def _pick(n, prefs):
    for p in prefs:
        if n % p == 0:
            return p
    return n


def kernel(x, c, positions, w_in, w_a, w_b, w_o, w_ada, b_ada, g_mix, g_ffn, w_up, conv_w,
           conv_b, w_down, g_final):
    batch, seq, d = x.shape
    depth = w_in.shape[0]
    hs = w_a.shape[1] // HEAD_DIM
    hd = w_b.shape[1] // HEAD_DIM
    rest = w_in.shape[2] - 3 * hs * HEAD_DIM - 3 * hd * HEAD_DIM - 2 * d - IDX_DIM
    hi = rest // (IDX_DIM + 1)
    dff = w_down.shape[1]
    m = batch * seq
    lay = _proj_layout(d, hs, hd, hi)

    tm = _pick(seq, (1024, 512, 256, 128))
    tm_mix = _pick(seq, (512, 256, 128))
    t_sb = _pick(seq, (256, 128))
    nh_sb = 2
    tq_dsa = _pick(seq, (256, 128))
    tn_ffn = _pick(dff, (512, 256, 128))
    tn_out = _pick(d, (512, 256, 128))

    w_in_r = _arrange_w_in(w_in, d, hs, hd, hi, lay["misc"][1])
    w_a16, w_b16, w_o16, w_up16, w_down16 = (t.astype(_BF16)
                                             for t in (w_a, w_b, w_o, w_up, w_down))
    tabs = _rope_tables(positions)
    mod = _modulation(c, w_ada, b_ada).reshape(depth, batch, N_MOD, 1, d)
    g_mix3, g_ffn3 = g_mix.reshape(depth, 1, d), g_ffn.reshape(depth, 1, d)
    conv_p = _conv_params(conv_w, conv_b, tn_ffn)

    x2 = x.reshape(m, d)
    for l in range(depth):
        sh1, sc1, gt1, sh2, sc2, gt2 = (mod[l, :, n] for n in range(N_MOD))
        proj, wi = _in_proj(x2, g_mix3, sc1, sh1, w_in_r, l, tabs, lay, hi, seq, tm)
        ya = _sb_attention(proj, lay, batch, seq, hs, t_sb, nh_sb)
        yb = _dsa_attention(proj, wi, lay, batch, seq, hi, hd, tq_dsa)
        x2 = _mix_out(ya, yb, w_a16, w_b16, w_o16, l, proj, x2, gt1, lay, seq, tm_mix)
        g = _ffn_up(x2, g_ffn3, sc2, sh2, w_up16, conv_p, l, seq, tm, tn_ffn)
        x2 = _out_res(g, w_down16, l, x2, gt2, seq, tm, tn_out)
    out = _final_norm(x2, g_final.reshape(1, d), tm)
    return out.reshape(batch, seq, d)
```

```python
import functools
import math

import jax
import jax.numpy as jnp
from jax import lax
from jax.experimental import pallas as pl
from jax.experimental.pallas import tpu as pltpu

HEAD_DIM = 128
IDX_DIM = 64
TOPK_MAX = 256
CONV_WIDTH = 3
ROPE_THETA = 10000.0
NORM_EPS = 1e-6
N_MOD = 6
LANES = 128
SUBLANES = 8
CHUNKS_PER_BLOCK = 4
PROJ_TN = LANES * CHUNKS_PER_BLOCK
PROJ_BLOCKS_PER_STEP = 2
FFN_BLOCKS_PER_STEP = 2
VMEM_LIMIT = 56 * 1024 * 1024
INT_MIN = -2 ** 31
I16_MIN, I16_MAX = -2 ** 15, 2 ** 15 - 1
PACK16 = 16
NEG_BIG = -1e30
ROW_CHUNK = 256
LOG2E = math.log2(math.e)

_F32 = jnp.float32
_BF16 = jnp.bfloat16
_NT = (((1,), (1,)), ((), ()))


def _cparams(n_axes):
    return pltpu.CompilerParams(dimension_semantics=("arbitrary",) * n_axes,
                                vmem_limit_bytes=VMEM_LIMIT)


def _mod_kernel(c_ref, w_ref, b_ref, o_ref):
    c = c_ref[...]
    c_act = (c * jax.nn.sigmoid(c)).astype(_BF16)
    acc = jnp.dot(c_act, w_ref[...].astype(_BF16), preferred_element_type=_F32)
    o_ref[...] = acc + b_ref[...]


def _modulation(c, w_ada, b_ada):
    depth, d, n = w_ada.shape
    b = c.shape[0]
    tn = 1024 if n % 1024 == 0 else n
    return pl.pallas_call(
        _mod_kernel,
        out_shape=jax.ShapeDtypeStruct((depth, b, n), _F32),
        grid=(depth, n // tn),
        in_specs=[
            pl.BlockSpec((b, d), lambda l, j: (0, 0)),
            pl.BlockSpec((None, d, tn), lambda l, j: (l, 0, j)),
            pl.BlockSpec((None, 1, tn), lambda l, j: (l, 0, j)),
        ],
        out_specs=pl.BlockSpec((None, b, tn), lambda l, j: (l, 0, j)),
        compiler_params=_cparams(2),
        name="adaln_mod",
    )(c, w_ada, b_ada.reshape(depth, 1, n))


def _norm_modulate(x, g, sc, sh):
    ms = jnp.mean(x * x, axis=-1, keepdims=True)
    r = x * lax.rsqrt(ms + NORM_EPS)
    return (r * g) * (1.0 + sc) + sh


def _proj_layout(d, hs, hd, hi):
    sizes = [("ga", d // LANES), ("gb", d // LANES),
             ("qa", hs), ("ka", hs), ("va", hs),
             ("qb", hd), ("kb", hd), ("vb", hd),
             ("qi", hi * IDX_DIM // LANES)]
    spb = PROJ_BLOCKS_PER_STEP
    if any(n % (spb * CHUNKS_PER_BLOCK) for _, n in sizes):
        spb = 1
    sizes.append(("misc", spb * CHUNKS_PER_BLOCK))
    off, lay = 0, {"spb": spb}
    for name, n in sizes:
        assert n % (spb * CHUNKS_PER_BLOCK) == 0, (name, n)
        lay[name] = (off, n)
        off += n
    lay["total"] = off
    return lay


def _rope128(t, cos, sin):
    return t * cos + pltpu.roll(t, HEAD_DIM // 2, axis=1) * sin


def _rope64(t, cos, sin_lo, sin_hi):
    half = IDX_DIM // 2
    return (t * cos + pltpu.roll(t, LANES - half, axis=1) * sin_lo
            + pltpu.roll(t, half, axis=1) * sin_hi)


def _in_proj_kernel(x_ref, g_ref, sc_ref, sh_ref, w_ref, c128_ref, s128_ref,
                    c64_ref, s64lo_ref, s64hi_ref, o_ref, wi_ref, h_ref, *, lay, idx_scale, tm):
    j = pl.program_id(1)

    rc = min(ROW_CHUNK, tm)

    @pl.when(j == 0)
    def _():
        def rows_chunk(r, c):
            rows = pl.ds(pl.multiple_of(r * rc, rc), rc)
            h = _norm_modulate(x_ref[rows, :], g_ref[...], sc_ref[...], sh_ref[...])
            h_ref[rows, :] = h.astype(_BF16)
            return c
        lax.fori_loop(0, tm // rc, rows_chunk, 0)

    spb = lay["spb"]
    step_chunks = spb * CHUNKS_PER_BLOCK
    blk = lambda name: (lay[name][0] // step_chunks, (lay[name][0] + lay[name][1]) // step_chunks)
    q_scale = (HEAD_DIM ** -0.5) * LOG2E

    def run(epilogue, nsub=spb):
        for sub in range(nsub):
            for r in range(tm // rc):
                rows = slice(r * rc, (r + 1) * rc)
                acc = jnp.dot(h_ref[rows, :], w_ref[:, sub * PROJ_TN:(sub + 1) * PROJ_TN],
                              preferred_element_type=_F32)
                epilogue(acc, rows, sub * CHUNKS_PER_BLOCK)

    def per_chunk(fn):
        def epilogue(acc, rows, c0):
            for c in range(CHUNKS_PER_BLOCK):
                o_ref[c0 + c, rows, :] = fn(acc[:, c * LANES:(c + 1) * LANES], rows).astype(_BF16)
        return epilogue

    def in_group(*names):
        cond = None
        for nm in names:
            lo, hi = blk(nm)
            t = jnp.logical_and(j >= lo, j < hi)
            cond = t if cond is None else jnp.logical_or(cond, t)
        return cond

    rope128 = lambda t, rows: _rope128(t, c128_ref[rows, :], s128_ref[rows, :])
    rope64 = lambda t, rows: _rope64(t, c64_ref[rows, :], s64lo_ref[rows, :], s64hi_ref[rows, :])

    @pl.when(in_group("ga", "gb"))
    def _():
        run(per_chunk(lambda t, rows: jax.nn.sigmoid(t)))

    @pl.when(in_group("qa", "ka", "va", "vb"))
    def _():
        scale = jnp.where(in_group("qa"), q_scale, 1.0)
        run(per_chunk(lambda t, rows: t * scale))

    @pl.when(in_group("qb", "kb"))
    def _():
        scale = jnp.where(in_group("qb"), q_scale, 1.0)
        run(per_chunk(lambda t, rows: rope128(t, rows) * scale))

    @pl.when(in_group("qi"))
    def _():
        run(per_chunk(rope64))

    @pl.when(in_group("misc"))
    def _():
        def epilogue(acc, rows, c0):
            for c in range(2):
                o_ref[c, rows, :] = rope64(acc[:, c * LANES:(c + 1) * LANES], rows).astype(_BF16)
            o_ref[2, rows, :] = acc[:, 2 * LANES:3 * LANES].astype(_BF16)
            wi_ref[rows, :] = acc[:, 2 * LANES:3 * LANES] * idx_scale
        run(epilogue, nsub=1)
        o_ref[3:step_chunks] = jnp.zeros((step_chunks - 3, tm, LANES), _BF16)


def _in_proj(x2, g, sc, sh, w, layer, tabs, lay, hi, seq, tm):
    m, d = x2.shape
    step_chunks = lay["spb"] * CHUNKS_PER_BLOCK
    nblk = lay["total"] // step_chunks
    tps = seq // tm
    row_spec = pl.BlockSpec((tm, LANES), lambda i, j: (i, 0))
    mod_spec = pl.BlockSpec((None, 1, d), lambda i, j: (i // tps, 0, 0))
    kern = functools.partial(_in_proj_kernel, lay=lay, tm=tm,
                             idx_scale=(IDX_DIM ** -0.5) * (hi ** -0.5))
    return pl.pallas_call(
        kern,
        out_shape=(jax.ShapeDtypeStruct((lay["total"], m, LANES), _BF16),
                   jax.ShapeDtypeStruct((m, LANES), _F32)),
        grid=(m // tm, nblk),
        in_specs=[
            pl.BlockSpec((tm, d), lambda i, j: (i, 0)),
            pl.BlockSpec((None, 1, d), lambda i, j: (layer, 0, 0)),
            mod_spec, mod_spec,
            pl.BlockSpec((None, d, step_chunks * LANES), lambda i, j: (layer, 0, j)),
            row_spec, row_spec, row_spec, row_spec, row_spec,
        ],
        out_specs=(pl.BlockSpec((step_chunks, tm, LANES), lambda i, j: (j, i, 0)),
                   pl.BlockSpec((tm, LANES), lambda i, j: (i, 0))),
        scratch_shapes=[pltpu.VMEM((tm, d), _BF16)],
        compiler_params=_cparams(2),
        name="in_proj",
    )(x2, g, sc, sh, w, *tabs)


def _add_lane_replicated(x, rep):
    n = x.shape[1] // LANES
    return jnp.concatenate([x[:, c * LANES:(c + 1) * LANES] + rep for c in range(n)], axis=1)


def _sb_kernel(q_ref, k_ref, v_ref, tri_ref, o_ref, *, t, nh, nq, qps):
    qi = pl.program_id(2)

    def variant(n, rows):
        past = (lax.broadcasted_iota(jnp.int32, (t, t), 1)
                < lax.broadcasted_iota(jnp.int32, (t, t), 0))

        def front(hh, kb):
            k = k_ref[hh, kb * t:(kb + 1) * t, :]
            z = lax.dot_general(q_ref[hh, rows, :], k, _NT,
                                preferred_element_type=_F32)
            sp = jnp.maximum(z, 0.0) + jnp.log(1.0 + jnp.exp2(-jnp.abs(z))) * LOG2E
            spm = (jnp.where(past, sp, 0.0) if kb == n else sp).astype(_BF16)
            cum = jnp.dot(spm, tri_ref[...], preferred_element_type=_F32)
            total = cum[:, 0:1] - spm[:, 0:1].astype(_F32)
            return z - sp, cum, total

        def back(hh, kb, zs, cum, carry):
            w = jnp.exp2(zs + _add_lane_replicated(cum, carry))
            if kb == n:
                w = jnp.where(past, w, 0.0)
            v = v_ref[hh, kb * t:(kb + 1) * t, :]
            return jnp.dot(w.astype(_BF16), v, preferred_element_type=_F32)

        items = [(hh, kb) for hh in range(nh) for kb in range(n, -1, -1)]
        ahead = front(*items[0])
        for idx, (hh, kb) in enumerate(items):
            zs, cum, total = ahead
            if idx + 1 < len(items):
                ahead = front(*items[idx + 1])
            if kb == n:
                acc = jnp.zeros((t, HEAD_DIM), _F32)
                carry = jnp.zeros((t, LANES), _F32)
            acc = acc + back(hh, kb, zs, cum, carry)
            carry = carry + jnp.broadcast_to(total, (t, LANES))
            if kb == 0:
                o_ref[rows, hh * HEAD_DIM:(hh + 1) * HEAD_DIM] = acc.astype(_BF16)

    def step(m):
        for sub in range(qps):
            variant(m * qps + sub, slice(sub * t, (sub + 1) * t))

    for m in range(nq // qps):
        pl.when(qi == m)(functools.partial(step, m))


def _sb_attention(proj, lay, batch, seq, hs, t, nh):
    m = batch * seq
    nq = seq // t
    qps = _pick(nq, (8, 4, 2, 1))
    ns = nq // qps
    qa, ka, va = (lay[n][0] // nh for n in ("qa", "ka", "va"))
    assert hs % nh == 0 and all(lay[n][0] % nh == 0 for n in ("qa", "ka", "va"))
    r = lax.broadcasted_iota(jnp.int32, (t, t), 0)
    c = lax.broadcasted_iota(jnp.int32, (t, t), 1)
    tri = jnp.where(r > c, -1.0, 0.0).astype(_BF16)
    return pl.pallas_call(
        functools.partial(_sb_kernel, t=t, nh=nh, nq=nq, qps=qps),
        out_shape=jax.ShapeDtypeStruct((m, hs * HEAD_DIM), _BF16),
        grid=(batch, hs // nh, ns),
        in_specs=[
            pl.BlockSpec((nh, qps * t, HEAD_DIM), lambda b, h, i: (qa + h, b * ns + i, 0)),
            pl.BlockSpec((nh, seq, HEAD_DIM), lambda b, h, i: (ka + h, b, 0)),
            pl.BlockSpec((nh, seq, HEAD_DIM), lambda b, h, i: (va + h, b, 0)),
            pl.BlockSpec((t, t), lambda b, h, i: (0, 0)),
        ],
        out_specs=pl.BlockSpec((qps * t, nh * HEAD_DIM), lambda b, h, i: (b * ns + i, h)),
        compiler_params=_cparams(3),
        name="sb_attn",
    )(proj, proj, proj, tri)


def _sortable(score):
    u = lax.bitcast_convert_type(score, jnp.int32)
    return u ^ (lax.shift_right_arithmetic(u, 31) & 0x7FFFFFFF)


def _dsa_kernel(qi_ref, ke_ref, wi_ref, qb_ref, kb_ref, vb_ref, lowtri_ref, o_ref,
                wt_ref, vt_ref, score_ref, key_ref, k16_ref, bias_ref, *, tq, seq, hi, hd, topk,
                nbuckets):
    i = pl.program_id(1)
    npairs = hi // 2
    tchunk = 256 if seq % 256 == 0 else LANES

    @pl.when(i == 0)
    def _():
        def xpose(h, c):
            for r in range(seq // tchunk):
                blk = vb_ref[h, r * tchunk:(r + 1) * tchunk, :].astype(_F32)
                vt_ref[h, :, r * tchunk:(r + 1) * tchunk] = blk.T.astype(_BF16)
            return c
        lax.fori_loop(0, hd, xpose, 0)

    wt_ref[...] = wi_ref[...].T

    def run(width):
        key_pos = lax.broadcasted_iota(jnp.int32, (width, tq), 0)
        q_pos = i * tq + lax.broadcasted_iota(jnp.int32, (width, tq), 1)
        visible = key_pos <= q_pos

        def pair(p):
            qp = qi_ref[p]
            d_even = lax.dot_general(ke_ref[0, :width, :], qp, _NT, preferred_element_type=_F32)
            d_odd = lax.dot_general(ke_ref[1, :width, :], qp, _NT, preferred_element_type=_F32)
            return (wt_ref[pl.ds(2 * p, 1), :] * jnp.maximum(d_even, 0.0)
                    + wt_ref[pl.ds(2 * p + 1, 1), :] * jnp.maximum(d_odd, 0.0))

        score_ref[:width, :] = jnp.zeros((width, tq), _F32)
        group = _pick(npairs, (4, 2, 1))

        def pairs(g, c):
            for u in range(group):
                score_ref[:width, :] += pair(g * group + u)
            return c

        lax.fori_loop(0, npairs // group, pairs, 0)
        key_ref[:width, :] = jnp.where(visible, _sortable(score_ref[:width, :]), INT_MIN)

        def count_ge16(cand):
            hit = jnp.where(k16_ref[:width, :] >= cand.astype(jnp.int16), jnp.int16(1), jnp.int16(0))
            parts = [hit[r * PACK16:(r + 1) * PACK16, :] for r in range(width // PACK16)]
            while len(parts) > 1:
                odd = parts[-1:] if len(parts) % 2 else []
                parts = [a + b for a, b in zip(parts[0::2], parts[1::2])] + odd
            return jnp.sum(parts[0].astype(jnp.int32), axis=0, keepdims=True)

        def search16(need):
            prefix = jnp.where(count_ge16(jnp.zeros((1, tq), jnp.int32)) >= need, 0, I16_MIN)

            def step(b, prefix):
                cand = prefix + lax.shift_left(jnp.int32(1), 14 - b)
                return jnp.where(count_ge16(cand) >= need, cand, prefix)

            return lax.fori_loop(0, 15, step, prefix.astype(jnp.int32))

        key_hi = lax.shift_right_arithmetic(key_ref[:width, :], 16)
        k16_ref[:width, :] = key_hi.astype(jnp.int16)
        t_hi = search16(jnp.full((1, tq), topk, jnp.int32))
        n_above = jnp.where(t_hi < I16_MAX, count_ge16(jnp.minimum(t_hi + 1, I16_MAX)), 0)
        key = key_ref[:width, :]
        key_lo = (key & 0xFFFF) + I16_MIN
        tied = lax.shift_right_arithmetic(key, 16) == t_hi
        k16_ref[:width, :] = jnp.where(tied, key_lo, I16_MIN).astype(jnp.int16)
        t_lo = search16(topk - n_above)
        thr = t_hi * 65536 + (t_lo - I16_MIN)
        thr = jnp.maximum(thr, INT_MIN + 1)
        selected = key_ref[:width, :] >= thr
        bias_ref[:width, :] = jnp.where(selected, 0.0, NEG_BIG)
        n_selected = jnp.sum(jnp.where(selected, 1.0, 0.0), axis=0, keepdims=True)

        @pl.when(jnp.max(n_selected) > topk)
        def _():
            key = key_ref[:width, :]
            above = key > thr
            need = topk - jnp.sum(jnp.where(above, 1.0, 0.0), axis=0, keepdims=True)

            def block(r, before):
                rows = pl.ds(pl.multiple_of(r * tq, tq), tq)
                kblk = key_ref[rows, :]
                tie = kblk == thr
                within = jnp.dot(lowtri_ref[...], jnp.where(tie, 1.0, 0.0).astype(_BF16),
                                 preferred_element_type=_F32)
                keep = jnp.logical_and(tie, within + before <= need)
                bias_ref[rows, :] = jnp.where(jnp.logical_or(kblk > thr, keep), 0.0, NEG_BIG)
                return before + within[tq - 1:tq, :]

            lax.fori_loop(0, width // tq, block, jnp.zeros((1, tq), _F32))

        def logits(h):
            s = lax.dot_general(kb_ref[h, :width, :], qb_ref[h], _NT, preferred_element_type=_F32)
            return s + bias_ref[:width, :]

        def finish(h, s):
            mx = jnp.max(s, axis=0, keepdims=True)
            p = jnp.exp2(s - mx)
            den = jnp.sum(p, axis=0, keepdims=True)
            ot = jnp.dot(vt_ref[h, :, :width], p.astype(_BF16), preferred_element_type=_F32)
            o_ref[h] = (ot / den).T.astype(_BF16)

        group_h = _pick(hd, (4, 2, 1))

        def heads(g, c):
            h0 = g * group_h
            s_next = logits(h0)
            for u in range(group_h):
                s_cur = s_next
                if u + 1 < group_h:
                    s_next = logits(h0 + u + 1)
                finish(h0 + u, s_cur)
            return c

        lax.fori_loop(0, hd // group_h, heads, 0)

    nq = seq // tq
    per = nq // nbuckets
    bucket = i // per
    for bi in range(nbuckets):
        pl.when(bucket == bi)(functools.partial(run, (bi + 1) * per * tq))


def _dsa_attention(proj, wi, lay, batch, seq, hi, hd, tq):
    m = batch * seq
    nq = seq // tq
    topk = min(TOPK_MAX, seq // 4)
    nbuckets = 4 if nq % 4 == 0 else 1
    npairs = hi // 2
    qi_b = lay["qi"][0] // npairs
    ke_b = lay["misc"][0] // 2
    qb_b, kb_b, vb_b = (lay[n][0] // hd for n in ("qb", "kb", "vb"))
    assert lay["qi"][0] % npairs == 0 and lay["misc"][0] % 2 == 0
    assert all(lay[n][0] % hd == 0 for n in ("qb", "kb", "vb"))
    kern = functools.partial(_dsa_kernel, tq=tq, seq=seq, hi=hi, hd=hd, topk=topk,
                             nbuckets=nbuckets)
    r = lax.broadcasted_iota(jnp.int32, (tq, tq), 0)
    c = lax.broadcasted_iota(jnp.int32, (tq, tq), 1)
    lowtri = jnp.where(c <= r, 1.0, 0.0).astype(_BF16)
    return pl.pallas_call(
        kern,
        out_shape=jax.ShapeDtypeStruct((hd, m, HEAD_DIM), _BF16),
        grid=(batch, nq),
        in_specs=[
            pl.BlockSpec((npairs, tq, LANES), lambda b, i: (qi_b, b * nq + i, 0)),
            pl.BlockSpec((2, seq, LANES), lambda b, i: (ke_b, b, 0)),
            pl.BlockSpec((tq, LANES), lambda b, i: (b * nq + i, 0)),
            pl.BlockSpec((hd, tq, HEAD_DIM), lambda b, i: (qb_b, b * nq + i, 0)),
            pl.BlockSpec((hd, seq, HEAD_DIM), lambda b, i: (kb_b, b, 0)),
            pl.BlockSpec((hd, seq, HEAD_DIM), lambda b, i: (vb_b, b, 0)),
            pl.BlockSpec((tq, tq), lambda b, i: (0, 0)),
        ],
        out_specs=pl.BlockSpec((hd, tq, HEAD_DIM), lambda b, i: (0, b * nq + i, 0)),
        scratch_shapes=[pltpu.VMEM((LANES, tq), _F32),
                        pltpu.VMEM((hd, HEAD_DIM, seq), _BF16),
                        pltpu.VMEM((seq, tq), _F32),
                        pltpu.VMEM((seq, tq), jnp.int32),
                        pltpu.VMEM((seq, tq), jnp.int16),
                        pltpu.VMEM((seq, tq), _F32)],
        compiler_params=_cparams(2),
        name="dsa_attn",
    )(proj, proj, wi, proj, proj, proj, lowtri)


def _mix_out_kernel(ya_ref, yb_ref, wa_ref, wb_ref, ga_ref, gb_ref, wo_ref, x_ref, gt_ref,
                    o_ref, mg_ref, *, hd, tm):
    rc = min(ROW_CHUNK, tm)
    for r in range(tm // rc):
        rows = slice(r * rc, (r + 1) * rc)
        yb = jnp.concatenate([yb_ref[h, rows, :] for h in range(hd)], axis=1)
        a = jnp.dot(ya_ref[rows, :], wa_ref[...], preferred_element_type=_F32)
        b = jnp.dot(yb, wb_ref[...], preferred_element_type=_F32)
        for c in range(a.shape[1] // LANES):
            sl = slice(c * LANES, (c + 1) * LANES)
            mg_ref[rows, sl] = (ga_ref[c, rows, :].astype(_F32) * a[:, sl]
                                + gb_ref[c, rows, :].astype(_F32) * b[:, sl]).astype(_BF16)
    for r in range(tm // rc):
        rows = slice(r * rc, (r + 1) * rc)
        acc = jnp.dot(mg_ref[rows, :], wo_ref[...], preferred_element_type=_F32)
        o_ref[rows, :] = x_ref[rows, :] + (1.0 + gt_ref[...]) * acc


def _mix_out(ya, yb, wa, wb, wo, layer, proj, x2, gt, lay, seq, tm):
    m, ka = ya.shape
    hd = yb.shape[0]
    d = wo.shape[2]
    nchunks = d // LANES
    tps = seq // tm
    assert lay["ga"][0] % nchunks == 0 and lay["gb"][0] % nchunks == 0
    gate = lambda name: pl.BlockSpec((nchunks, tm, LANES),
                                     lambda i: (lay[name][0] // nchunks, i, 0))
    resident = lambda rows: pl.BlockSpec((None, rows, d), lambda i: (layer, 0, 0),
                                         pipeline_mode=pl.Buffered(1))
    return pl.pallas_call(
        functools.partial(_mix_out_kernel, hd=hd, tm=tm),
        out_shape=jax.ShapeDtypeStruct((m, d), _F32),
        grid=(m // tm,),
        in_specs=[
            pl.BlockSpec((tm, ka), lambda i: (i, 0)),
            pl.BlockSpec((hd, tm, HEAD_DIM), lambda i: (0, i, 0)),
            resident(ka), resident(hd * HEAD_DIM),
            gate("ga"), gate("gb"),
            resident(d),
            pl.BlockSpec((tm, d), lambda i: (i, 0)),
            pl.BlockSpec((None, 1, d), lambda i: (i // tps, 0, 0)),
        ],
        out_specs=pl.BlockSpec((tm, d), lambda i: (i, 0)),
        scratch_shapes=[pltpu.VMEM((tm, d), _BF16)],
        compiler_params=_cparams(1),
        name="mix_out",
    )(ya, yb, wa, wb, proj, proj, wo, x2, gt)


def _out_res_kernel(a_ref, w_ref, x_ref, gt_ref, o_ref):
    acc = jnp.dot(a_ref[...], w_ref[...], preferred_element_type=_F32)
    o_ref[...] = x_ref[...] + (1.0 + gt_ref[...]) * acc


def _out_res(a, w, layer, x2, gt, seq, tm, tn):
    m, k = a.shape
    d = w.shape[2]
    tps = seq // tm
    return pl.pallas_call(
        _out_res_kernel,
        out_shape=jax.ShapeDtypeStruct((m, d), _F32),
        grid=(m // tm, d // tn),
        in_specs=[
            pl.BlockSpec((tm, k), lambda i, j: (i, 0)),
            pl.BlockSpec((None, k, tn), lambda i, j: (layer, 0, j)),
            pl.BlockSpec((tm, tn), lambda i, j: (i, j)),
            pl.BlockSpec((None, 1, tn), lambda i, j: (i // tps, 0, j)),
        ],
        out_specs=pl.BlockSpec((tm, tn), lambda i, j: (i, j)),
        compiler_params=_cparams(2),
        name="out_res",
    )(a, w, x2, gt)


def _shift_rows(u, k, prev):
    rolled = pltpu.roll(u, k, axis=0)
    first = lax.broadcasted_iota(jnp.int32, prev.shape, 0) < k
    head = jnp.where(first, pltpu.roll(prev, k, axis=0), rolled[:SUBLANES])
    return jnp.concatenate([head, rolled[SUBLANES:]], axis=0)


def _ffn_up_kernel(x_ref, g_ref, sc_ref, sh_ref, *refs, tm, tn, tps, cps, nblk):
    w_refs, (cp_ref, o_ref, h_ref, halo_ref, carry_ref) = refs[:2 * cps], refs[2 * cps:]
    i = pl.program_id(0)
    j = pl.program_id(1)
    rc = min(ROW_CHUNK, tm)

    @pl.when(j == 0)
    def _():
        h = _norm_modulate(x_ref[...], g_ref[...], sc_ref[...], sh_ref[...])
        h_ref[...] = h.astype(_BF16)

    def column_block(sub):
        jj = j * cps + sub
        wa_ref, wb_ref = w_refs[2 * sub], w_refs[2 * sub + 1]

        @pl.when(i % tps == 0)
        def _():
            halo_ref[...] = jnp.zeros_like(halo_ref)

        @pl.when(i % tps != 0)
        def _():
            halo_ref[...] = carry_ref[jj]

        cp = cp_ref[jj]

        def conv(u, prev, cols):
            u1 = _shift_rows(u, 1, prev)
            u2 = _shift_rows(u, 2, prev)
            return cp[3:4, cols] + (cp[0:1, cols] * u2 + cp[1:2, cols] * u1 + cp[2:3, cols] * u)

        prev_a = halo_ref[:, 0:tn]
        prev_b = halo_ref[:, tn:2 * tn]
        for r in range(tm // rc):
            lo = r * rc
            hr = h_ref[lo:lo + rc, :]
            ua = jnp.dot(hr, wa_ref[...], preferred_element_type=_F32)
            ub = jnp.dot(hr, wb_ref[...], preferred_element_type=_F32)
            a = conv(ua, prev_a, slice(0, tn))
            b = conv(ub, prev_b, slice(tn, 2 * tn))
            o_ref[lo:lo + rc, sub * tn:(sub + 1) * tn] = (a * jax.nn.sigmoid(a) * b).astype(_BF16)
            prev_a = ua[rc - SUBLANES:, :]
            prev_b = ub[rc - SUBLANES:, :]

        carry_ref[jj] = jnp.concatenate([prev_a, prev_b], axis=1)

    for sub in range(cps):
        if sub == 0 or nblk % cps == 0:
            column_block(sub)
        else:
            pl.when(j * cps + sub < nblk)(functools.partial(column_block, sub))


def _conv_params(conv_w, conv_b, tn):
    depth, taps, width = conv_w.shape
    nblk = width // (2 * tn)
    rows = jnp.concatenate([conv_w, conv_b[:, None, :],
                            jnp.zeros((depth, SUBLANES - taps - 1, width), conv_w.dtype)], axis=1)
    rows = rows.reshape(depth, SUBLANES, 2, nblk, tn)
    return jnp.transpose(rows, (0, 3, 1, 2, 4)).reshape(depth, nblk, SUBLANES, 2 * tn)


def _ffn_up(x2, g, sc, sh, w, cp, layer, seq, tm, tn):
    m, d = x2.shape
    nblk = w.shape[2] // (2 * tn)
    cps = FFN_BLOCKS_PER_STEP
    tps = seq // tm
    mod_spec = pl.BlockSpec((None, 1, d), lambda i, j: (i // tps, 0, 0))

    def halves(sub):
        col = lambda j: jnp.minimum(j * cps + sub, nblk - 1)
        return (pl.BlockSpec((None, d, tn), lambda i, j: (layer, 0, col(j))),
                pl.BlockSpec((None, d, tn), lambda i, j: (layer, 0, nblk + col(j))))

    w_specs = [s for sub in range(cps) for s in halves(sub)]
    return pl.pallas_call(
        functools.partial(_ffn_up_kernel, tm=tm, tn=tn, tps=tps, cps=cps, nblk=nblk),
        out_shape=jax.ShapeDtypeStruct((m, nblk * tn), _BF16),
        grid=(m // tm, pl.cdiv(nblk, cps)),
        in_specs=[
            pl.BlockSpec((tm, d), lambda i, j: (i, 0)),
            pl.BlockSpec((None, 1, d), lambda i, j: (layer, 0, 0)),
            mod_spec, mod_spec,
            *w_specs,
            pl.BlockSpec((None, nblk, SUBLANES, 2 * tn), lambda i, j: (layer, 0, 0, 0)),
        ],
        out_specs=pl.BlockSpec((tm, cps * tn), lambda i, j: (i, j)),
        scratch_shapes=[pltpu.VMEM((tm, d), _BF16),
                        pltpu.VMEM((SUBLANES, 2 * tn), _F32),
                        pltpu.VMEM((nblk, SUBLANES, 2 * tn), _F32)],
        compiler_params=_cparams(2),
        name="ffn_up",
    )(x2, g, sc, sh, *([w] * (2 * cps)), cp)


def _final_norm_kernel(x_ref, g_ref, o_ref):
    x = x_ref[...]
    ms = jnp.mean(x * x, axis=-1, keepdims=True)
    o_ref[...] = (x * lax.rsqrt(ms + NORM_EPS)) * g_ref[...]


def _final_norm(x2, g, tm):
    m, d = x2.shape
    return pl.pallas_call(
        _final_norm_kernel,
        out_shape=jax.ShapeDtypeStruct((m, d), _F32),
        grid=(m // tm,),
        in_specs=[pl.BlockSpec((tm, d), lambda i: (i, 0)),
                  pl.BlockSpec((1, d), lambda i: (0, 0))],
        out_specs=pl.BlockSpec((tm, d), lambda i: (i, 0)),
        compiler_params=_cparams(1),
        name="final_norm",
    )(x2, g)


def _arrange_w_in(w_in, d, hs, hd, hi, misc_chunks):
    sw, dw, iw = hs * HEAD_DIM, hd * HEAD_DIM, hi * IDX_DIM
    sizes = (sw, sw, sw, dw, dw, dw, iw, IDX_DIM, hi, d, d)
    offs = [0]
    for s in sizes:
        offs.append(offs[-1] + s)
    qa, ka, va, qb, kb, vb, qi, ki, wi, ga, gb = (
        w_in[..., offs[n]:offs[n + 1]] for n in range(len(sizes)))
    z = lambda n: jnp.zeros(w_in.shape[:-1] + (n,), w_in.dtype)
    misc = jnp.concatenate([ki, z(LANES - IDX_DIM), z(LANES - IDX_DIM), ki,
                            wi, z(LANES - hi), z((misc_chunks - 3) * LANES)], axis=-1)
    return jnp.concatenate([ga, gb, qa, ka, va, qb, kb, vb, qi, misc], axis=-1).astype(_BF16)


def _rope_tables(positions):
    pos = positions.reshape(-1).astype(_F32)

    def cos_sin(dim):
        inv_freq = 1.0 / (ROPE_THETA ** (jnp.arange(0, dim, 2, dtype=_F32) / dim))
        ang = pos[:, None] * inv_freq
        return jnp.cos(ang), jnp.sin(ang)

    c, s = cos_sin(HEAD_DIM)
    c128 = jnp.concatenate([c, c], axis=-1)
    s128 = jnp.concatenate([-s, s], axis=-1)
    c, s = cos_sin(IDX_DIM)
    zero = jnp.zeros_like(s)
    c64 = jnp.concatenate([c, c, c, c], axis=-1)
    s64lo = jnp.concatenate([-s, zero, -s, zero], axis=-1)
    s64hi = jnp.concatenate([zero, s, zero, s], axis=-1)
    return c128, s128, c64, s64lo, s64hi


def _pick(n, prefs):
    for p in prefs:
        if n % p == 0:
            return p
    return n


def kernel(x, c, positions, w_in, w_a, w_b, w_o, w_ada, b_ada, g_mix, g_ffn, w_up, conv_w,
           conv_b, w_down, g_final):
    batch, seq, d = x.shape
    depth = w_in.shape[0]
    hs = w_a.shape[1] // HEAD_DIM
    hd = w_b.shape[1] // HEAD_DIM
    rest = w_in.shape[2] - 3 * hs * HEAD_DIM - 3 * hd * HEAD_DIM - 2 * d - IDX_DIM
    hi = rest // (IDX_DIM + 1)
    dff = w_down.shape[1]
    m = batch * seq
    lay = _proj_layout(d, hs, hd, hi)

    tm = _pick(seq, (1024, 512, 256, 128))
    tm_mix = _pick(seq, (512, 256, 128))
    t_sb = _pick(seq, (256, 128))
    nh_sb = 2
    tq_dsa = _pick(seq, (256, 128))
    tn_ffn = _pick(dff, (512, 256, 128))
    tn_out = _pick(d, (512, 256, 128))

    w_in_r = _arrange_w_in(w_in, d, hs, hd, hi, lay["misc"][1])
    w_a16, w_b16, w_o16, w_up16, w_down16 = (t.astype(_BF16)
                                             for t in (w_a, w_b, w_o, w_up, w_down))
    tabs = _rope_tables(positions)
    mod = _modulation(c, w_ada, b_ada).reshape(depth, batch, N_MOD, 1, d)
    g_mix3, g_ffn3 = g_mix.reshape(depth, 1, d), g_ffn.reshape(depth, 1, d)
    conv_p = _conv_params(conv_w, conv_b, tn_ffn)

    x2 = x.reshape(m, d)
    for l in range(depth):
        sh1, sc1, gt1, sh2, sc2, gt2 = (mod[l, :, n] for n in range(N_MOD))
        proj, wi = _in_proj(x2, g_mix3, sc1, sh1, w_in_r, l, tabs, lay, hi, seq, tm)
        ya = _sb_attention(proj, lay, batch, seq, hs, t_sb, nh_sb)
        yb = _dsa_attention(proj, wi, lay, batch, seq, hi, hd, tq_dsa)
        x2 = _mix_out(ya, yb, w_a16, w_b16, w_o16, l, proj, x2, gt1, lay, seq, tm_mix)
        g = _ffn_up(x2, g_ffn3, sc2, sh2, w_up16, conv_p, l, seq, tm, tn_ffn)
        x2 = _out_res(g, w_down16, l, x2, gt2, seq, tm, tn_out)
    out = _final_norm(x2, g_final.reshape(1, d), tm)
    return out.reshape(batch, seq, d)
```

```python
import functools
import math

import jax
import jax.numpy as jnp
from jax import lax
from jax.experimental import pallas as pl
from jax.experimental.pallas import tpu as pltpu

HEAD_DIM = 128
IDX_DIM = 64
TOPK_MAX = 256
CONV_WIDTH = 3
ROPE_THETA = 10000.0
NORM_EPS = 1e-6
N_MOD = 6
LANES = 128
SUBLANES = 8
CHUNKS_PER_BLOCK = 4
PROJ_TN = LANES * CHUNKS_PER_BLOCK
PROJ_BLOCKS_PER_STEP = 2
FFN_BLOCKS_PER_STEP = 2
FFN_ROW_CHUNK = 128
VMEM_LIMIT = 56 * 1024 * 1024
INT_MIN = -2 ** 31
I16_MIN, I16_MAX = -2 ** 15, 2 ** 15 - 1
PACK16 = 16
NEG_BIG = -1e30
ROW_CHUNK = 256
LOG2E = math.log2(math.e)

_F32 = jnp.float32
_BF16 = jnp.bfloat16
_NT = (((1,), (1,)), ((), ()))


def _cparams(n_axes):
    return pltpu.CompilerParams(dimension_semantics=("arbitrary",) * n_axes,
                                vmem_limit_bytes=VMEM_LIMIT)


def _mod_kernel(c_ref, w_ref, b_ref, o_ref):
    c = c_ref[...]
    c_act = (c * jax.nn.sigmoid(c)).astype(_BF16)
    acc = jnp.dot(c_act, w_ref[...].astype(_BF16), preferred_element_type=_F32)
    o_ref[...] = acc + b_ref[...]


def _modulation(c, w_ada, b_ada):
    depth, d, n = w_ada.shape
    b = c.shape[0]
    tn = 1024 if n % 1024 == 0 else n
    return pl.pallas_call(
        _mod_kernel,
        out_shape=jax.ShapeDtypeStruct((depth, b, n), _F32),
        grid=(depth, n // tn),
        in_specs=[
            pl.BlockSpec((b, d), lambda l, j: (0, 0)),
            pl.BlockSpec((None, d, tn), lambda l, j: (l, 0, j)),
            pl.BlockSpec((None, 1, tn), lambda l, j: (l, 0, j)),
        ],
        out_specs=pl.BlockSpec((None, b, tn), lambda l, j: (l, 0, j)),
        compiler_params=_cparams(2),
        name="adaln_mod",
    )(c, w_ada, b_ada.reshape(depth, 1, n))


def _norm_modulate(x, g, sc, sh):
    ms = jnp.mean(x * x, axis=-1, keepdims=True)
    r = x * lax.rsqrt(ms + NORM_EPS)
    return (r * g) * (1.0 + sc) + sh


def _proj_layout(d, hs, hd, hi):
    sizes = [("ga", d // LANES), ("gb", d // LANES),
             ("qa", hs), ("ka", hs), ("va", hs),
             ("qb", hd), ("kb", hd), ("vb", hd),
             ("qi", hi * IDX_DIM // LANES)]
    spb = PROJ_BLOCKS_PER_STEP
    if any(n % (spb * CHUNKS_PER_BLOCK) for _, n in sizes):
        spb = 1
    sizes.append(("misc", spb * CHUNKS_PER_BLOCK))
    off, lay = 0, {"spb": spb}
    for name, n in sizes:
        assert n % (spb * CHUNKS_PER_BLOCK) == 0, (name, n)
        lay[name] = (off, n)
        off += n
    lay["total"] = off
    return lay


def _rope128(t, cos, sin):
    return t * cos + pltpu.roll(t, HEAD_DIM // 2, axis=1) * sin


def _rope64(t, cos, sin_lo, sin_hi):
    half = IDX_DIM // 2
    return (t * cos + pltpu.roll(t, LANES - half, axis=1) * sin_lo
            + pltpu.roll(t, half, axis=1) * sin_hi)


def _in_proj_kernel(x_ref, g_ref, sc_ref, sh_ref, w_ref, c128_ref, s128_ref,
                    c64_ref, s64lo_ref, s64hi_ref, o_ref, wi_ref, h_ref, *, lay, idx_scale, tm):
    j = pl.program_id(1)

    rc = min(ROW_CHUNK, tm)

    @pl.when(j == 0)
    def _():
        def rows_chunk(r, c):
            rows = pl.ds(pl.multiple_of(r * rc, rc), rc)
            h = _norm_modulate(x_ref[rows, :], g_ref[...], sc_ref[...], sh_ref[...])
            h_ref[rows, :] = h.astype(_BF16)
            return c
        lax.fori_loop(0, tm // rc, rows_chunk, 0)

    spb = lay["spb"]
    step_chunks = spb * CHUNKS_PER_BLOCK
    blk = lambda name: (lay[name][0] // step_chunks, (lay[name][0] + lay[name][1]) // step_chunks)
    q_scale = (HEAD_DIM ** -0.5) * LOG2E

    def run(epilogue, nsub=spb):
        for sub in range(nsub):
            for r in range(tm // rc):
                rows = slice(r * rc, (r + 1) * rc)
                acc = jnp.dot(h_ref[rows, :], w_ref[:, sub * PROJ_TN:(sub + 1) * PROJ_TN],
                              preferred_element_type=_F32)
                epilogue(acc, rows, sub * CHUNKS_PER_BLOCK)

    def per_chunk(fn):
        def epilogue(acc, rows, c0):
            for c in range(CHUNKS_PER_BLOCK):
                o_ref[c0 + c, rows, :] = fn(acc[:, c * LANES:(c + 1) * LANES], rows).astype(_BF16)
        return epilogue

    def in_group(*names):
        cond = None
        for nm in names:
            lo, hi = blk(nm)
            t = jnp.logical_and(j >= lo, j < hi)
            cond = t if cond is None else jnp.logical_or(cond, t)
        return cond

    rope128 = lambda t, rows: _rope128(t, c128_ref[rows, :], s128_ref[rows, :])
    rope64 = lambda t, rows: _rope64(t, c64_ref[rows, :], s64lo_ref[rows, :], s64hi_ref[rows, :])

    @pl.when(in_group("ga", "gb"))
    def _():
        run(per_chunk(lambda t, rows: jax.nn.sigmoid(t)))

    @pl.when(in_group("qa", "ka", "va", "vb"))
    def _():
        scale = jnp.where(in_group("qa"), q_scale, 1.0)
        run(per_chunk(lambda t, rows: t * scale))

    @pl.when(in_group("qb", "kb"))
    def _():
        scale = jnp.where(in_group("qb"), q_scale, 1.0)
        run(per_chunk(lambda t, rows: rope128(t, rows) * scale))

    @pl.when(in_group("qi"))
    def _():
        run(per_chunk(rope64))

    @pl.when(in_group("misc"))
    def _():
        def epilogue(acc, rows, c0):
            for c in range(2):
                o_ref[c, rows, :] = rope64(acc[:, c * LANES:(c + 1) * LANES], rows).astype(_BF16)
            o_ref[2, rows, :] = acc[:, 2 * LANES:3 * LANES].astype(_BF16)
            wi_ref[rows, :] = acc[:, 2 * LANES:3 * LANES] * idx_scale
        run(epilogue, nsub=1)
        o_ref[3:step_chunks] = jnp.zeros((step_chunks - 3, tm, LANES), _BF16)


def _in_proj(x2, g, sc, sh, w, layer, tabs, lay, hi, seq, tm):
    m, d = x2.shape
    step_chunks = lay["spb"] * CHUNKS_PER_BLOCK
    nblk = lay["total"] // step_chunks
    tps = seq // tm
    row_spec = pl.BlockSpec((tm, LANES), lambda i, j: (i, 0))
    mod_spec = pl.BlockSpec((None, 1, d), lambda i, j: (i // tps, 0, 0))
    kern = functools.partial(_in_proj_kernel, lay=lay, tm=tm,
                             idx_scale=(IDX_DIM ** -0.5) * (hi ** -0.5))
    return pl.pallas_call(
        kern,
        out_shape=(jax.ShapeDtypeStruct((lay["total"], m, LANES), _BF16),
                   jax.ShapeDtypeStruct((m, LANES), _F32)),
        grid=(m // tm, nblk),
        in_specs=[
            pl.BlockSpec((tm, d), lambda i, j: (i, 0)),
            pl.BlockSpec((None, 1, d), lambda i, j: (layer, 0, 0)),
            mod_spec, mod_spec,
            pl.BlockSpec((None, d, step_chunks * LANES), lambda i, j: (layer, 0, j)),
            row_spec, row_spec, row_spec, row_spec, row_spec,
        ],
        out_specs=(pl.BlockSpec((step_chunks, tm, LANES), lambda i, j: (j, i, 0)),
                   pl.BlockSpec((tm, LANES), lambda i, j: (i, 0))),
        scratch_shapes=[pltpu.VMEM((tm, d), _BF16)],
        compiler_params=_cparams(2),
        name="in_proj",
    )(x2, g, sc, sh, w, *tabs)


def _add_lane_replicated(x, rep):
    n = x.shape[1] // LANES
    return jnp.concatenate([x[:, c * LANES:(c + 1) * LANES] + rep for c in range(n)], axis=1)


def _sb_kernel(q_ref, k_ref, v_ref, tri_ref, o_ref, *, t, nh, nq, qps):
    qi = pl.program_id(2)

    def variant(n, rows):
        past = (lax.broadcasted_iota(jnp.int32, (t, t), 1)
                < lax.broadcasted_iota(jnp.int32, (t, t), 0))

        def front(hh, kb):
            k = k_ref[hh, kb * t:(kb + 1) * t, :]
            z = lax.dot_general(q_ref[hh, rows, :], k, _NT,
                                preferred_element_type=_F32)
            sp = jnp.maximum(z, 0.0) + jnp.log(1.0 + jnp.exp2(-jnp.abs(z))) * LOG2E
            spm = (jnp.where(past, sp, 0.0) if kb == n else sp).astype(_BF16)
            cum = jnp.dot(spm, tri_ref[...], preferred_element_type=_F32)
            total = cum[:, 0:1] - spm[:, 0:1].astype(_F32)
            return z - sp, cum, total

        def back(hh, kb, zs, cum, carry):
            w = jnp.exp2(zs + _add_lane_replicated(cum, carry))
            if kb == n:
                w = jnp.where(past, w, 0.0)
            v = v_ref[hh, kb * t:(kb + 1) * t, :]
            return jnp.dot(w.astype(_BF16), v, preferred_element_type=_F32)

        items = [(hh, kb) for hh in range(nh) for kb in range(n, -1, -1)]
        ahead = front(*items[0])
        for idx, (hh, kb) in enumerate(items):
            zs, cum, total = ahead
            if idx + 1 < len(items):
                ahead = front(*items[idx + 1])
            if kb == n:
                acc = jnp.zeros((t, HEAD_DIM), _F32)
                carry = jnp.zeros((t, LANES), _F32)
            acc = acc + back(hh, kb, zs, cum, carry)
            carry = carry + jnp.broadcast_to(total, (t, LANES))
            if kb == 0:
                o_ref[rows, hh * HEAD_DIM:(hh + 1) * HEAD_DIM] = acc.astype(_BF16)

    def step(m):
        for sub in range(qps):
            variant(m * qps + sub, slice(sub * t, (sub + 1) * t))

    for m in range(nq // qps):
        pl.when(qi == m)(functools.partial(step, m))


def _sb_attention(proj, lay, batch, seq, hs, t, nh):
    m = batch * seq
    nq = seq // t
    qps = _pick(nq, (8, 4, 2, 1))
    ns = nq // qps
    qa, ka, va = (lay[n][0] // nh for n in ("qa", "ka", "va"))
    assert hs % nh == 0 and all(lay[n][0] % nh == 0 for n in ("qa", "ka", "va"))
    r = lax.broadcasted_iota(jnp.int32, (t, t), 0)
    c = lax.broadcasted_iota(jnp.int32, (t, t), 1)
    tri = jnp.where(r > c, -1.0, 0.0).astype(_BF16)
    return pl.pallas_call(
        functools.partial(_sb_kernel, t=t, nh=nh, nq=nq, qps=qps),
        out_shape=jax.ShapeDtypeStruct((m, hs * HEAD_DIM), _BF16),
        grid=(batch, hs // nh, ns),
        in_specs=[
            pl.BlockSpec((nh, qps * t, HEAD_DIM), lambda b, h, i: (qa + h, b * ns + i, 0)),
            pl.BlockSpec((nh, seq, HEAD_DIM), lambda b, h, i: (ka + h, b, 0)),
            pl.BlockSpec((nh, seq, HEAD_DIM), lambda b, h, i: (va + h, b, 0)),
            pl.BlockSpec((t, t), lambda b, h, i: (0, 0)),
        ],
        out_specs=pl.BlockSpec((qps * t, nh * HEAD_DIM), lambda b, h, i: (b * ns + i, h)),
        compiler_params=_cparams(3),
        name="sb_attn",
    )(proj, proj, proj, tri)


def _sortable(score):
    u = lax.bitcast_convert_type(score, jnp.int32)
    return u ^ (lax.shift_right_arithmetic(u, 31) & 0x7FFFFFFF)


def _dsa_kernel(qi_ref, ke_ref, wi_ref, qb_ref, kb_ref, vb_ref, lowtri_ref, o_ref,
                wt_ref, vt_ref, score_ref, key_ref, k16_ref, bias_ref, *, tq, seq, hi, hd, topk,
                nbuckets):
    i = pl.program_id(1)
    npairs = hi // 2
    tchunk = 256 if seq % 256 == 0 else LANES

    @pl.when(i == 0)
    def _():
        def xpose(h, c):
            for r in range(seq // tchunk):
                blk = vb_ref[h, r * tchunk:(r + 1) * tchunk, :].astype(_F32)
                vt_ref[h, :, r * tchunk:(r + 1) * tchunk] = blk.T.astype(_BF16)
            return c
        lax.fori_loop(0, hd, xpose, 0)

    wt_ref[...] = wi_ref[...].T

    def run(width):
        key_pos = lax.broadcasted_iota(jnp.int32, (width, tq), 0)
        q_pos = i * tq + lax.broadcasted_iota(jnp.int32, (width, tq), 1)
        visible = key_pos <= q_pos

        def pair(p):
            qp = qi_ref[p]
            d_even = lax.dot_general(ke_ref[0, :width, :], qp, _NT, preferred_element_type=_F32)
            d_odd = lax.dot_general(ke_ref[1, :width, :], qp, _NT, preferred_element_type=_F32)
            return (wt_ref[pl.ds(2 * p, 1), :] * jnp.maximum(d_even, 0.0)
                    + wt_ref[pl.ds(2 * p + 1, 1), :] * jnp.maximum(d_odd, 0.0))

        score_ref[:width, :] = jnp.zeros((width, tq), _F32)
        group = _pick(npairs, (4, 2, 1))

        def pairs(g, c):
            for u in range(group):
                score_ref[:width, :] += pair(g * group + u)
            return c

        lax.fori_loop(0, npairs // group, pairs, 0)
        key_ref[:width, :] = jnp.where(visible, _sortable(score_ref[:width, :]), INT_MIN)

        def count_ge16(cand):
            hit = jnp.where(k16_ref[:width, :] >= cand.astype(jnp.int16), jnp.int16(1), jnp.int16(0))
            parts = [hit[r * PACK16:(r + 1) * PACK16, :] for r in range(width // PACK16)]
            while len(parts) > 1:
                odd = parts[-1:] if len(parts) % 2 else []
                parts = [a + b for a, b in zip(parts[0::2], parts[1::2])] + odd
            return jnp.sum(parts[0].astype(jnp.int32), axis=0, keepdims=True)

        def search16(need):
            prefix = jnp.where(count_ge16(jnp.zeros((1, tq), jnp.int32)) >= need, 0, I16_MIN)

            def step(b, prefix):
                cand = prefix + lax.shift_left(jnp.int32(1), 14 - b)
                return jnp.where(count_ge16(cand) >= need, cand, prefix)

            return lax.fori_loop(0, 15, step, prefix.astype(jnp.int32))

        key_hi = lax.shift_right_arithmetic(key_ref[:width, :], 16)
        k16_ref[:width, :] = key_hi.astype(jnp.int16)
        t_hi = search16(jnp.full((1, tq), topk, jnp.int32))
        n_above = jnp.where(t_hi < I16_MAX, count_ge16(jnp.minimum(t_hi + 1, I16_MAX)), 0)
        key = key_ref[:width, :]
        key_lo = (key & 0xFFFF) + I16_MIN
        tied = lax.shift_right_arithmetic(key, 16) == t_hi
        k16_ref[:width, :] = jnp.where(tied, key_lo, I16_MIN).astype(jnp.int16)
        t_lo = search16(topk - n_above)
        thr = t_hi * 65536 + (t_lo - I16_MIN)
        thr = jnp.maximum(thr, INT_MIN + 1)
        selected = key_ref[:width, :] >= thr
        bias_ref[:width, :] = jnp.where(selected, 0.0, NEG_BIG)
        n_selected = jnp.sum(jnp.where(selected, 1.0, 0.0), axis=0, keepdims=True)

        @pl.when(jnp.max(n_selected) > topk)
        def _():
            key = key_ref[:width, :]
            above = key > thr
            need = topk - jnp.sum(jnp.where(above, 1.0, 0.0), axis=0, keepdims=True)

            def block(r, before):
                rows = pl.ds(pl.multiple_of(r * tq, tq), tq)
                kblk = key_ref[rows, :]
                tie = kblk == thr
                within = jnp.dot(lowtri_ref[...], jnp.where(tie, 1.0, 0.0).astype(_BF16),
                                 preferred_element_type=_F32)
                keep = jnp.logical_and(tie, within + before <= need)
                bias_ref[rows, :] = jnp.where(jnp.logical_or(kblk > thr, keep), 0.0, NEG_BIG)
                return before + within[tq - 1:tq, :]

            lax.fori_loop(0, width // tq, block, jnp.zeros((1, tq), _F32))

        def logits(h):
            s = lax.dot_general(kb_ref[h, :width, :], qb_ref[h], _NT, preferred_element_type=_F32)
            return s + bias_ref[:width, :]

        def finish(h, s):
            mx = jnp.max(s, axis=0, keepdims=True)
            p = jnp.exp2(s - mx)
            den = jnp.sum(p, axis=0, keepdims=True)
            ot = jnp.dot(vt_ref[h, :, :width], p.astype(_BF16), preferred_element_type=_F32)
            o_ref[h] = (ot / den).T.astype(_BF16)

        group_h = _pick(hd, (4, 2, 1))

        def heads(g, c):
            h0 = g * group_h
            s_next = logits(h0)
            for u in range(group_h):
                s_cur = s_next
                if u + 1 < group_h:
                    s_next = logits(h0 + u + 1)
                finish(h0 + u, s_cur)
            return c

        lax.fori_loop(0, hd // group_h, heads, 0)

    nq = seq // tq
    per = nq // nbuckets
    bucket = i // per
    for bi in range(nbuckets):
        pl.when(bucket == bi)(functools.partial(run, (bi + 1) * per * tq))


def _dsa_attention(proj, wi, lay, batch, seq, hi, hd, tq):
    m = batch * seq
    nq = seq // tq
    topk = min(TOPK_MAX, seq // 4)
    nbuckets = 4 if nq % 4 == 0 else 1
    npairs = hi // 2
    qi_b = lay["qi"][0] // npairs
    ke_b = lay["misc"][0] // 2
    qb_b, kb_b, vb_b = (lay[n][0] // hd for n in ("qb", "kb", "vb"))
    assert lay["qi"][0] % npairs == 0 and lay["misc"][0] % 2 == 0
    assert all(lay[n][0] % hd == 0 for n in ("qb", "kb", "vb"))
    kern = functools.partial(_dsa_kernel, tq=tq, seq=seq, hi=hi, hd=hd, topk=topk,
                             nbuckets=nbuckets)
    r = lax.broadcasted_iota(jnp.int32, (tq, tq), 0)
    c = lax.broadcasted_iota(jnp.int32, (tq, tq), 1)
    lowtri = jnp.where(c <= r, 1.0, 0.0).astype(_BF16)
    return pl.pallas_call(
        kern,
        out_shape=jax.ShapeDtypeStruct((hd, m, HEAD_DIM), _BF16),
        grid=(batch, nq),
        in_specs=[
            pl.BlockSpec((npairs, tq, LANES), lambda b, i: (qi_b, b * nq + i, 0)),
            pl.BlockSpec((2, seq, LANES), lambda b, i: (ke_b, b, 0)),
            pl.BlockSpec((tq, LANES), lambda b, i: (b * nq + i, 0)),
            pl.BlockSpec((hd, tq, HEAD_DIM), lambda b, i: (qb_b, b * nq + i, 0)),
            pl.BlockSpec((hd, seq, HEAD_DIM), lambda b, i: (kb_b, b, 0)),
            pl.BlockSpec((hd, seq, HEAD_DIM), lambda b, i: (vb_b, b, 0)),
            pl.BlockSpec((tq, tq), lambda b, i: (0, 0)),
        ],
        out_specs=pl.BlockSpec((hd, tq, HEAD_DIM), lambda b, i: (0, b * nq + i, 0)),
        scratch_shapes=[pltpu.VMEM((LANES, tq), _F32),
                        pltpu.VMEM((hd, HEAD_DIM, seq), _BF16),
                        pltpu.VMEM((seq, tq), _F32),
                        pltpu.VMEM((seq, tq), jnp.int32),
                        pltpu.VMEM((seq, tq), jnp.int16),
                        pltpu.VMEM((seq, tq), _F32)],
        compiler_params=_cparams(2),
        name="dsa_attn",
    )(proj, proj, wi, proj, proj, proj, lowtri)


def _mix_out_kernel(ya_ref, yb_ref, wa_ref, wb_ref, ga_ref, gb_ref, wo_ref, x_ref, gt_ref,
                    o_ref, mg_ref, *, hd, tm):
    rc = min(ROW_CHUNK, tm)
    for r in range(tm // rc):
        rows = slice(r * rc, (r + 1) * rc)
        yb = jnp.concatenate([yb_ref[h, rows, :] for h in range(hd)], axis=1)
        a = jnp.dot(ya_ref[rows, :], wa_ref[...], preferred_element_type=_F32)
        b = jnp.dot(yb, wb_ref[...], preferred_element_type=_F32)
        for c in range(a.shape[1] // LANES):
            sl = slice(c * LANES, (c + 1) * LANES)
            mg_ref[rows, sl] = (ga_ref[c, rows, :].astype(_F32) * a[:, sl]
                                + gb_ref[c, rows, :].astype(_F32) * b[:, sl]).astype(_BF16)
    for r in range(tm // rc):
        rows = slice(r * rc, (r + 1) * rc)
        acc = jnp.dot(mg_ref[rows, :], wo_ref[...], preferred_element_type=_F32)
        o_ref[rows, :] = x_ref[rows, :] + (1.0 + gt_ref[...]) * acc


def _mix_out(ya, yb, wa, wb, wo, layer, proj, x2, gt, lay, seq, tm):
    m, ka = ya.shape
    hd = yb.shape[0]
    d = wo.shape[2]
    nchunks = d // LANES
    tps = seq // tm
    assert lay["ga"][0] % nchunks == 0 and lay["gb"][0] % nchunks == 0
    gate = lambda name: pl.BlockSpec((nchunks, tm, LANES),
                                     lambda i: (lay[name][0] // nchunks, i, 0))
    resident = lambda rows: pl.BlockSpec((None, rows, d), lambda i: (layer, 0, 0),
                                         pipeline_mode=pl.Buffered(1))
    return pl.pallas_call(
        functools.partial(_mix_out_kernel, hd=hd, tm=tm),
        out_shape=jax.ShapeDtypeStruct((m, d), _F32),
        grid=(m // tm,),
        in_specs=[
            pl.BlockSpec((tm, ka), lambda i: (i, 0)),
            pl.BlockSpec((hd, tm, HEAD_DIM), lambda i: (0, i, 0)),
            resident(ka), resident(hd * HEAD_DIM),
            gate("ga"), gate("gb"),
            resident(d),
            pl.BlockSpec((tm, d), lambda i: (i, 0)),
            pl.BlockSpec((None, 1, d), lambda i: (i // tps, 0, 0)),
        ],
        out_specs=pl.BlockSpec((tm, d), lambda i: (i, 0)),
        scratch_shapes=[pltpu.VMEM((tm, d), _BF16)],
        compiler_params=_cparams(1),
        name="mix_out",
    )(ya, yb, wa, wb, proj, proj, wo, x2, gt)


def _out_res_kernel(a_ref, w_ref, x_ref, gt_ref, o_ref):
    acc = jnp.dot(a_ref[...], w_ref[...], preferred_element_type=_F32)
    o_ref[...] = x_ref[...] + (1.0 + gt_ref[...]) * acc


def _out_res(a, w, layer, x2, gt, seq, tm, tn):
    m, k = a.shape
    d = w.shape[2]
    tps = seq // tm
    return pl.pallas_call(
        _out_res_kernel,
        out_shape=jax.ShapeDtypeStruct((m, d), _F32),
        grid=(m // tm, d // tn),
        in_specs=[
            pl.BlockSpec((tm, k), lambda i, j: (i, 0)),
            pl.BlockSpec((None, k, tn), lambda i, j: (layer, 0, j)),
            pl.BlockSpec((tm, tn), lambda i, j: (i, j)),
            pl.BlockSpec((None, 1, tn), lambda i, j: (i // tps, 0, j)),
        ],
        out_specs=pl.BlockSpec((tm, tn), lambda i, j: (i, j)),
        compiler_params=_cparams(2),
        name="out_res",
    )(a, w, x2, gt)


def _shift_rows(u, k, prev):
    rolled = pltpu.roll(u, k, axis=0)
    first = lax.broadcasted_iota(jnp.int32, prev.shape, 0) < k
    head = jnp.where(first, pltpu.roll(prev, k, axis=0), rolled[:SUBLANES])
    return jnp.concatenate([head, rolled[SUBLANES:]], axis=0)


def _ffn_up_kernel(x_ref, g_ref, sc_ref, sh_ref, *refs, tm, tn, tps, cps, nblk):
    w_refs, (cp_ref, o_ref, h_ref, halo_ref, carry_ref) = refs[:2 * cps], refs[2 * cps:]
    i = pl.program_id(0)
    j = pl.program_id(1)
    rc = min(FFN_ROW_CHUNK, tm)

    @pl.when(j == 0)
    def _():
        h = _norm_modulate(x_ref[...], g_ref[...], sc_ref[...], sh_ref[...])
        h_ref[...] = h.astype(_BF16)

    def column_block(sub):
        jj = j * cps + sub
        wa_ref, wb_ref = w_refs[2 * sub], w_refs[2 * sub + 1]

        @pl.when(i % tps == 0)
        def _():
            halo_ref[...] = jnp.zeros_like(halo_ref)

        @pl.when(i % tps != 0)
        def _():
            halo_ref[...] = carry_ref[jj]

        cp = cp_ref[jj]

        def conv(u, prev, cols):
            u1 = _shift_rows(u, 1, prev)
            u2 = _shift_rows(u, 2, prev)
            return cp[3:4, cols] + (cp[0:1, cols] * u2 + cp[1:2, cols] * u1 + cp[2:3, cols] * u)

        prev_a = halo_ref[:, 0:tn]
        prev_b = halo_ref[:, tn:2 * tn]
        for r in range(tm // rc):
            lo = r * rc
            hr = h_ref[lo:lo + rc, :]
            ua = jnp.dot(hr, wa_ref[...], preferred_element_type=_F32)
            ub = jnp.dot(hr, wb_ref[...], preferred_element_type=_F32)
            a = conv(ua, prev_a, slice(0, tn))
            b = conv(ub, prev_b, slice(tn, 2 * tn))
            o_ref[lo:lo + rc, sub * tn:(sub + 1) * tn] = (a * jax.nn.sigmoid(a) * b).astype(_BF16)
            prev_a = ua[rc - SUBLANES:, :]
            prev_b = ub[rc - SUBLANES:, :]

        carry_ref[jj] = jnp.concatenate([prev_a, prev_b], axis=1)

    for sub in range(cps):
        if sub == 0 or nblk % cps == 0:
            column_block(sub)
        else:
            pl.when(j * cps + sub < nblk)(functools.partial(column_block, sub))


def _conv_params(conv_w, conv_b, tn):
    depth, taps, width = conv_w.shape
    nblk = width // (2 * tn)
    rows = jnp.concatenate([conv_w, conv_b[:, None, :],
                            jnp.zeros((depth, SUBLANES - taps - 1, width), conv_w.dtype)], axis=1)
    rows = rows.reshape(depth, SUBLANES, 2, nblk, tn)
    return jnp.transpose(rows, (0, 3, 1, 2, 4)).reshape(depth, nblk, SUBLANES, 2 * tn)


def _ffn_up(x2, g, sc, sh, w, cp, layer, seq, tm, tn):
    m, d = x2.shape
    nblk = w.shape[2] // (2 * tn)
    cps = FFN_BLOCKS_PER_STEP
    tps = seq // tm
    mod_spec = pl.BlockSpec((None, 1, d), lambda i, j: (i // tps, 0, 0))

    def halves(sub):
        col = lambda j: jnp.minimum(j * cps + sub, nblk - 1)
        return (pl.BlockSpec((None, d, tn), lambda i, j: (layer, 0, col(j))),
                pl.BlockSpec((None, d, tn), lambda i, j: (layer, 0, nblk + col(j))))

    w_specs = [s for sub in range(cps) for s in halves(sub)]
    return pl.pallas_call(
        functools.partial(_ffn_up_kernel, tm=tm, tn=tn, tps=tps, cps=cps, nblk=nblk),
        out_shape=jax.ShapeDtypeStruct((m, nblk * tn), _BF16),
        grid=(m // tm, pl.cdiv(nblk, cps)),
        in_specs=[
            pl.BlockSpec((tm, d), lambda i, j: (i, 0)),
            pl.BlockSpec((None, 1, d), lambda i, j: (layer, 0, 0)),
            mod_spec, mod_spec,
            *w_specs,
            pl.BlockSpec((None, nblk, SUBLANES, 2 * tn), lambda i, j: (layer, 0, 0, 0)),
        ],
        out_specs=pl.BlockSpec((tm, cps * tn), lambda i, j: (i, j)),
        scratch_shapes=[pltpu.VMEM((tm, d), _BF16),
                        pltpu.VMEM((SUBLANES, 2 * tn), _F32),
                        pltpu.VMEM((nblk, SUBLANES, 2 * tn), _F32)],
        compiler_params=_cparams(2),
        name="ffn_up",
    )(x2, g, sc, sh, *([w] * (2 * cps)), cp)


def _final_norm_kernel(x_ref, g_ref, o_ref):
    x = x_ref[...]
    ms = jnp.mean(x * x, axis=-1, keepdims=True)
    o_ref[...] = (x * lax.rsqrt(ms + NORM_EPS)) * g_ref[...]


def _final_norm(x2, g, tm):
    m, d = x2.shape
    return pl.pallas_call(
        _final_norm_kernel,
        out_shape=jax.ShapeDtypeStruct((m, d), _F32),
        grid=(m // tm,),
        in_specs=[pl.BlockSpec((tm, d), lambda i: (i, 0)),
                  pl.BlockSpec((1, d), lambda i: (0, 0))],
        out_specs=pl.BlockSpec((tm, d), lambda i: (i, 0)),
        compiler_params=_cparams(1),
        name="final_norm",
    )(x2, g)


def _arrange_w_in(w_in, d, hs, hd, hi, misc_chunks):
    sw, dw, iw = hs * HEAD_DIM, hd * HEAD_DIM, hi * IDX_DIM
    sizes = (sw, sw, sw, dw, dw, dw, iw, IDX_DIM, hi, d, d)
    offs = [0]
    for s in sizes:
        offs.append(offs[-1] + s)
    qa, ka, va, qb, kb, vb, qi, ki, wi, ga, gb = (
        w_in[..., offs[n]:offs[n + 1]] for n in range(len(sizes)))
    z = lambda n: jnp.zeros(w_in.shape[:-1] + (n,), w_in.dtype)
    misc = jnp.concatenate([ki, z(LANES - IDX_DIM), z(LANES - IDX_DIM), ki,
                            wi, z(LANES - hi), z((misc_chunks - 3) * LANES)], axis=-1)
    return jnp.concatenate([ga, gb, qa, ka, va, qb, kb, vb, qi, misc], axis=-1).astype(_BF16)


def _rope_tables(positions):
    pos = positions.reshape(-1).astype(_F32)

    def cos_sin(dim):
        inv_freq = 1.0 / (ROPE_THETA ** (jnp.arange(0, dim, 2, dtype=_F32) / dim))
        ang = pos[:, None] * inv_freq
        return jnp.cos(ang), jnp.sin(ang)

    c, s = cos_sin(HEAD_DIM)
    c128 = jnp.concatenate([c, c], axis=-1)
    s128 = jnp.concatenate([-s, s], axis=-1)
    c, s = cos_sin(IDX_DIM)
    zero = jnp.zeros_like(s)
    c64 = jnp.concatenate([c, c, c, c], axis=-1)
    s64lo = jnp.concatenate([-s, zero, -s, zero], axis=-1)
    s64hi = jnp.concatenate([zero, s, zero, s], axis=-1)
    return c128, s128, c64, s64lo, s64hi


def _pick(n, prefs):
    for p in prefs:
        if n % p == 0:
            return p
    return n


def kernel(x, c, positions, w_in, w_a, w_b, w_o, w_ada, b_ada, g_mix, g_ffn, w_up, conv_w,
           conv_b, w_down, g_final):
    batch, seq, d = x.shape
    depth = w_in.shape[0]
    hs = w_a.shape[1] // HEAD_DIM
    hd = w_b.shape[1] // HEAD_DIM
    rest = w_in.shape[2] - 3 * hs * HEAD_DIM - 3 * hd * HEAD_DIM - 2 * d - IDX_DIM
    hi = rest // (IDX_DIM + 1)
    dff = w_down.shape[1]
    m = batch * seq
    lay = _proj_layout(d, hs, hd, hi)

    tm = _pick(seq, (1024, 512, 256, 128))
    tm_mix = _pick(seq, (512, 256, 128))
    t_sb = _pick(seq, (256, 128))
    nh_sb = 2
    tq_dsa = _pick(seq, (256, 128))
    tn_ffn = _pick(dff, (512, 256, 128))
    tn_out = _pick(d, (512, 256, 128))

    w_in_r = _arrange_w_in(w_in, d, hs, hd, hi, lay["misc"][1])
    w_a16, w_b16, w_o16, w_up16, w_down16 = (t.astype(_BF16)
                                             for t in (w_a, w_b, w_o, w_up, w_down))
    tabs = _rope_tables(positions)
    mod = _modulation(c, w_ada, b_ada).reshape(depth, batch, N_MOD, 1, d)
    g_mix3, g_ffn3 = g_mix.reshape(depth, 1, d), g_ffn.reshape(depth, 1, d)
    conv_p = _conv_params(conv_w, conv_b, tn_ffn)

    x2 = x.reshape(m, d)
    for l in range(depth):
        sh1, sc1, gt1, sh2, sc2, gt2 = (mod[l, :, n] for n in range(N_MOD))
        proj, wi = _in_proj(x2, g_mix3, sc1, sh1, w_in_r, l, tabs, lay, hi, seq, tm)
        ya = _sb_attention(proj, lay, batch, seq, hs, t_sb, nh_sb)
        yb = _dsa_attention(proj, wi, lay, batch, seq, hi, hd, tq_dsa)
        x2 = _mix_out(ya, yb, w_a16, w_b16, w_o16, l, proj, x2, gt1, lay, seq, tm_mix)
        g = _ffn_up(x2, g_ffn3, sc2, sh2, w_up16, conv_p, l, seq, tm, tn_ffn)
        x2 = _out_res(g, w_down16, l, x2, gt2, seq, tm, tn_out)
    out = _final_norm(x2, g_final.reshape(1, d), tm)
    return out.reshape(batch, seq, d)
```

```python
import functools
import math

import jax
import jax.numpy as jnp
from jax import lax
from jax.experimental import pallas as pl
from jax.experimental.pallas import tpu as pltpu

HEAD_DIM = 128
IDX_DIM = 64
TOPK_MAX = 256
CONV_WIDTH = 3
ROPE_THETA = 10000.0
NORM_EPS = 1e-6
N_MOD = 6
LANES = 128
SUBLANES = 8
CHUNKS_PER_BLOCK = 4
PROJ_TN = LANES * CHUNKS_PER_BLOCK
PROJ_BLOCKS_PER_STEP = 2
FFN_BLOCKS_PER_STEP = 2
FFN_ROW_CHUNK = 128
VMEM_LIMIT = 56 * 1024 * 1024
INT_MIN = -2 ** 31
I16_MIN, I16_MAX = -2 ** 15, 2 ** 15 - 1
PACK16 = 16
NEG_BIG = -1e30
ROW_CHUNK = 256
LOG2E = math.log2(math.e)

_F32 = jnp.float32
_BF16 = jnp.bfloat16
_NT = (((1,), (1,)), ((), ()))


def _cparams(n_axes):
    return pltpu.CompilerParams(dimension_semantics=("arbitrary",) * n_axes,
                                vmem_limit_bytes=VMEM_LIMIT)


def _mod_kernel(c_ref, w_ref, b_ref, o_ref):
    c = c_ref[...]
    c_act = (c * jax.nn.sigmoid(c)).astype(_BF16)
    acc = jnp.dot(c_act, w_ref[...].astype(_BF16), preferred_element_type=_F32)
    o_ref[...] = acc + b_ref[...]


def _modulation(c, w_ada, b_ada):
    depth, d, n = w_ada.shape
    b = c.shape[0]
    tn = 1024 if n % 1024 == 0 else n
    return pl.pallas_call(
        _mod_kernel,
        out_shape=jax.ShapeDtypeStruct((depth, b, n), _F32),
        grid=(depth, n // tn),
        in_specs=[
            pl.BlockSpec((b, d), lambda l, j: (0, 0)),
            pl.BlockSpec((None, d, tn), lambda l, j: (l, 0, j)),
            pl.BlockSpec((None, 1, tn), lambda l, j: (l, 0, j)),
        ],
        out_specs=pl.BlockSpec((None, b, tn), lambda l, j: (l, 0, j)),
        compiler_params=_cparams(2),
        name="adaln_mod",
    )(c, w_ada, b_ada.reshape(depth, 1, n))


def _norm_modulate(x, g, sc, sh):
    ms = jnp.mean(x * x, axis=-1, keepdims=True)
    r = x * lax.rsqrt(ms + NORM_EPS)
    return (r * g) * (1.0 + sc) + sh


def _proj_layout(d, hs, hd, hi):
    sizes = [("ga", d // LANES), ("gb", d // LANES),
             ("qa", hs), ("ka", hs), ("va", hs),
             ("qb", hd), ("kb", hd), ("vb", hd),
             ("qi", hi * IDX_DIM // LANES)]
    spb = PROJ_BLOCKS_PER_STEP
    if any(n % (spb * CHUNKS_PER_BLOCK) for _, n in sizes):
        spb = 1
    sizes.append(("misc", spb * CHUNKS_PER_BLOCK))
    off, lay = 0, {"spb": spb}
    for name, n in sizes:
        assert n % (spb * CHUNKS_PER_BLOCK) == 0, (name, n)
        lay[name] = (off, n)
        off += n
    lay["total"] = off
    return lay


def _rope128(t, cos, sin):
    return t * cos + pltpu.roll(t, HEAD_DIM // 2, axis=1) * sin


def _rope64(t, cos, sin_lo, sin_hi):
    half = IDX_DIM // 2
    return (t * cos + pltpu.roll(t, LANES - half, axis=1) * sin_lo
            + pltpu.roll(t, half, axis=1) * sin_hi)


def _in_proj_kernel(x_ref, g_ref, sc_ref, sh_ref, w_ref, c128_ref, s128_ref,
                    c64_ref, s64lo_ref, s64hi_ref, o_ref, wi_ref, h_ref, *, lay, idx_scale, tm):
    j = pl.program_id(1)

    rc = min(ROW_CHUNK, tm)

    @pl.when(j == 0)
    def _():
        def rows_chunk(r, c):
            rows = pl.ds(pl.multiple_of(r * rc, rc), rc)
            h = _norm_modulate(x_ref[rows, :], g_ref[...], sc_ref[...], sh_ref[...])
            h_ref[rows, :] = h.astype(_BF16)
            return c
        lax.fori_loop(0, tm // rc, rows_chunk, 0)

    spb = lay["spb"]
    step_chunks = spb * CHUNKS_PER_BLOCK
    blk = lambda name: (lay[name][0] // step_chunks, (lay[name][0] + lay[name][1]) // step_chunks)
    q_scale = (HEAD_DIM ** -0.5) * LOG2E

    def run(epilogue, nsub=spb):
        for sub in range(nsub):
            for r in range(tm // rc):
                rows = slice(r * rc, (r + 1) * rc)
                acc = jnp.dot(h_ref[rows, :], w_ref[:, sub * PROJ_TN:(sub + 1) * PROJ_TN],
                              preferred_element_type=_F32)
                epilogue(acc, rows, sub * CHUNKS_PER_BLOCK)

    def per_chunk(fn):
        def epilogue(acc, rows, c0):
            for c in range(CHUNKS_PER_BLOCK):
                o_ref[c0 + c, rows, :] = fn(acc[:, c * LANES:(c + 1) * LANES], rows).astype(_BF16)
        return epilogue

    def in_group(*names):
        cond = None
        for nm in names:
            lo, hi = blk(nm)
            t = jnp.logical_and(j >= lo, j < hi)
            cond = t if cond is None else jnp.logical_or(cond, t)
        return cond

    rope128 = lambda t, rows: _rope128(t, c128_ref[rows, :], s128_ref[rows, :])
    rope64 = lambda t, rows: _rope64(t, c64_ref[rows, :], s64lo_ref[rows, :], s64hi_ref[rows, :])

    @pl.when(in_group("ga", "gb"))
    def _():
        run(per_chunk(lambda t, rows: jax.nn.sigmoid(t)))

    @pl.when(in_group("qa", "ka", "va", "vb"))
    def _():
        scale = jnp.where(in_group("qa"), q_scale, 1.0)
        run(per_chunk(lambda t, rows: t * scale))

    @pl.when(in_group("qb", "kb"))
    def _():
        scale = jnp.where(in_group("qb"), q_scale, 1.0)
        run(per_chunk(lambda t, rows: rope128(t, rows) * scale))

    @pl.when(in_group("qi"))
    def _():
        run(per_chunk(rope64))

    @pl.when(in_group("misc"))
    def _():
        def epilogue(acc, rows, c0):
            for c in range(2):
                o_ref[c, rows, :] = rope64(acc[:, c * LANES:(c + 1) * LANES], rows).astype(_BF16)
            o_ref[2, rows, :] = acc[:, 2 * LANES:3 * LANES].astype(_BF16)
            wi_ref[rows, :] = acc[:, 2 * LANES:3 * LANES] * idx_scale
        run(epilogue, nsub=1)
        o_ref[3:step_chunks] = jnp.zeros((step_chunks - 3, tm, LANES), _BF16)


def _in_proj(x2, g, sc, sh, w, layer, tabs, lay, hi, seq, tm):
    m, d = x2.shape
    step_chunks = lay["spb"] * CHUNKS_PER_BLOCK
    nblk = lay["total"] // step_chunks
    tps = seq // tm
    row_spec = pl.BlockSpec((tm, LANES), lambda i, j: (i, 0))
    mod_spec = pl.BlockSpec((None, 1, d), lambda i, j: (i // tps, 0, 0))
    kern = functools.partial(_in_proj_kernel, lay=lay, tm=tm,
                             idx_scale=(IDX_DIM ** -0.5) * (hi ** -0.5))
    return pl.pallas_call(
        kern,
        out_shape=(jax.ShapeDtypeStruct((lay["total"], m, LANES), _BF16),
                   jax.ShapeDtypeStruct((m, LANES), _F32)),
        grid=(m // tm, nblk),
        in_specs=[
            pl.BlockSpec((tm, d), lambda i, j: (i, 0)),
            pl.BlockSpec((None, 1, d), lambda i, j: (layer, 0, 0)),
            mod_spec, mod_spec,
            pl.BlockSpec((None, d, step_chunks * LANES), lambda i, j: (layer, 0, j)),
            row_spec, row_spec, row_spec, row_spec, row_spec,
        ],
        out_specs=(pl.BlockSpec((step_chunks, tm, LANES), lambda i, j: (j, i, 0)),
                   pl.BlockSpec((tm, LANES), lambda i, j: (i, 0))),
        scratch_shapes=[pltpu.VMEM((tm, d), _BF16)],
        compiler_params=_cparams(2),
        name="in_proj",
    )(x2, g, sc, sh, w, *tabs)


def _add_lane_replicated(x, rep):
    n = x.shape[1] // LANES
    return jnp.concatenate([x[:, c * LANES:(c + 1) * LANES] + rep for c in range(n)], axis=1)


def _sb_kernel(q_ref, k_ref, v_ref, tri_ref, o_ref, *, t, nh, nq, qps):
    qi = pl.program_id(2)

    def variant(n, rows):
        past = (lax.broadcasted_iota(jnp.int32, (t, t), 1)
                < lax.broadcasted_iota(jnp.int32, (t, t), 0))

        def front(hh, kb):
            k = k_ref[hh, kb * t:(kb + 1) * t, :]
            z = lax.dot_general(q_ref[hh, rows, :], k, _NT,
                                preferred_element_type=_F32)
            sp = jnp.maximum(z, 0.0) + jnp.log(1.0 + jnp.exp2(-jnp.abs(z))) * LOG2E
            spm = (jnp.where(past, sp, 0.0) if kb == n else sp).astype(_BF16)
            cum = jnp.dot(spm, tri_ref[...], preferred_element_type=_F32)
            total = cum[:, 0:1] - spm[:, 0:1].astype(_F32)
            return z - sp, cum, total

        def back(hh, kb, zs, cum, carry):
            w = jnp.exp2(zs + _add_lane_replicated(cum, carry))
            if kb == n:
                w = jnp.where(past, w, 0.0)
            v = v_ref[hh, kb * t:(kb + 1) * t, :]
            return jnp.dot(w.astype(_BF16), v, preferred_element_type=_F32)

        items = [(hh, kb) for hh in range(nh) for kb in range(n, -1, -1)]
        ahead = front(*items[0])
        for idx, (hh, kb) in enumerate(items):
            zs, cum, total = ahead
            if idx + 1 < len(items):
                ahead = front(*items[idx + 1])
            if kb == n:
                acc = jnp.zeros((t, HEAD_DIM), _F32)
                carry = jnp.zeros((t, LANES), _F32)
            acc = acc + back(hh, kb, zs, cum, carry)
            carry = carry + jnp.broadcast_to(total, (t, LANES))
            if kb == 0:
                o_ref[rows, hh * HEAD_DIM:(hh + 1) * HEAD_DIM] = acc.astype(_BF16)

    def step(m):
        for sub in range(qps):
            variant(m * qps + sub, slice(sub * t, (sub + 1) * t))

    for m in range(nq // qps):
        pl.when(qi == m)(functools.partial(step, m))


def _sb_attention(proj, lay, batch, seq, hs, t, nh):
    m = batch * seq
    nq = seq // t
    qps = _pick(nq, (8, 4, 2, 1))
    ns = nq // qps
    qa, ka, va = (lay[n][0] // nh for n in ("qa", "ka", "va"))
    assert hs % nh == 0 and all(lay[n][0] % nh == 0 for n in ("qa", "ka", "va"))
    r = lax.broadcasted_iota(jnp.int32, (t, t), 0)
    c = lax.broadcasted_iota(jnp.int32, (t, t), 1)
    tri = jnp.where(r > c, -1.0, 0.0).astype(_BF16)
    return pl.pallas_call(
        functools.partial(_sb_kernel, t=t, nh=nh, nq=nq, qps=qps),
        out_shape=jax.ShapeDtypeStruct((m, hs * HEAD_DIM), _BF16),
        grid=(batch, hs // nh, ns),
        in_specs=[
            pl.BlockSpec((nh, qps * t, HEAD_DIM), lambda b, h, i: (qa + h, b * ns + i, 0)),
            pl.BlockSpec((nh, seq, HEAD_DIM), lambda b, h, i: (ka + h, b, 0)),
            pl.BlockSpec((nh, seq, HEAD_DIM), lambda b, h, i: (va + h, b, 0)),
            pl.BlockSpec((t, t), lambda b, h, i: (0, 0)),
        ],
        out_specs=pl.BlockSpec((qps * t, nh * HEAD_DIM), lambda b, h, i: (b * ns + i, h)),
        compiler_params=_cparams(3),
        name="sb_attn",
    )(proj, proj, proj, tri)


def _sortable(score):
    u = lax.bitcast_convert_type(score, jnp.int32)
    return u ^ (lax.shift_right_arithmetic(u, 31) & 0x7FFFFFFF)


def _dsa_kernel(qi_ref, ke_ref, wi_ref, qb_ref, kb_ref, vb_ref, lowtri_ref, o_ref,
                wt_ref, vt_ref, score_ref, key_ref, k16_ref, bias_ref, *, tq, seq, hi, hd, topk,
                nbuckets):
    i = pl.program_id(1)
    npairs = hi // 2
    tchunk = 256 if seq % 256 == 0 else LANES

    @pl.when(i == 0)
    def _():
        def xpose(h, c):
            for r in range(seq // tchunk):
                blk = vb_ref[h, r * tchunk:(r + 1) * tchunk, :].astype(_F32)
                vt_ref[h, :, r * tchunk:(r + 1) * tchunk] = blk.T.astype(_BF16)
            return c
        lax.fori_loop(0, hd, xpose, 0)

    wt_ref[...] = wi_ref[...].T

    def run(width):
        key_pos = lax.broadcasted_iota(jnp.int32, (width, tq), 0)
        q_pos = i * tq + lax.broadcasted_iota(jnp.int32, (width, tq), 1)
        visible = key_pos <= q_pos

        def pair(p):
            qp = qi_ref[p]
            d_even = lax.dot_general(ke_ref[0, :width, :], qp, _NT, preferred_element_type=_F32)
            d_odd = lax.dot_general(ke_ref[1, :width, :], qp, _NT, preferred_element_type=_F32)
            return (wt_ref[pl.ds(2 * p, 1), :] * jnp.maximum(d_even, 0.0)
                    + wt_ref[pl.ds(2 * p + 1, 1), :] * jnp.maximum(d_odd, 0.0))

        score_ref[:width, :] = jnp.zeros((width, tq), _F32)
        group = _pick(npairs, (4, 2, 1))

        def pairs(g, c):
            for u in range(group):
                score_ref[:width, :] += pair(g * group + u)
            return c

        lax.fori_loop(0, npairs // group, pairs, 0)
        key_ref[:width, :] = jnp.where(visible, _sortable(score_ref[:width, :]), INT_MIN)

        def count_ge16(cand):
            hit = jnp.where(k16_ref[:width, :] >= cand.astype(jnp.int16), jnp.int16(1), jnp.int16(0))
            parts = [hit[r * PACK16:(r + 1) * PACK16, :] for r in range(width // PACK16)]
            while len(parts) > 1:
                odd = parts[-1:] if len(parts) % 2 else []
                parts = [a + b for a, b in zip(parts[0::2], parts[1::2])] + odd
            return jnp.sum(parts[0].astype(jnp.int32), axis=0, keepdims=True)

        def search16(need):
            prefix = jnp.where(count_ge16(jnp.zeros((1, tq), jnp.int32)) >= need, 0, I16_MIN)

            def step(b, prefix):
                cand = prefix + lax.shift_left(jnp.int32(1), 14 - b)
                return jnp.where(count_ge16(cand) >= need, cand, prefix)

            return lax.fori_loop(0, 15, step, prefix.astype(jnp.int32))

        key_hi = lax.shift_right_arithmetic(key_ref[:width, :], 16)
        k16_ref[:width, :] = key_hi.astype(jnp.int16)
        t_hi = search16(jnp.full((1, tq), topk, jnp.int32))
        n_above = jnp.where(t_hi < I16_MAX, count_ge16(jnp.minimum(t_hi + 1, I16_MAX)), 0)
        key = key_ref[:width, :]
        key_lo = (key & 0xFFFF) + I16_MIN
        tied = lax.shift_right_arithmetic(key, 16) == t_hi
        k16_ref[:width, :] = jnp.where(tied, key_lo, I16_MIN).astype(jnp.int16)
        t_lo = search16(topk - n_above)
        thr = t_hi * 65536 + (t_lo - I16_MIN)
        thr = jnp.maximum(thr, INT_MIN + 1)
        selected = key_ref[:width, :] >= thr
        bias_ref[:width, :] = jnp.where(selected, 0.0, NEG_BIG)
        n_selected = jnp.sum(jnp.where(selected, 1.0, 0.0), axis=0, keepdims=True)

        @pl.when(jnp.max(n_selected) > topk)
        def _():
            key = key_ref[:width, :]
            above = key > thr
            need = topk - jnp.sum(jnp.where(above, 1.0, 0.0), axis=0, keepdims=True)

            def block(r, before):
                rows = pl.ds(pl.multiple_of(r * tq, tq), tq)
                kblk = key_ref[rows, :]
                tie = kblk == thr
                within = jnp.dot(lowtri_ref[...], jnp.where(tie, 1.0, 0.0).astype(_BF16),
                                 preferred_element_type=_F32)
                keep = jnp.logical_and(tie, within + before <= need)
                bias_ref[rows, :] = jnp.where(jnp.logical_or(kblk > thr, keep), 0.0, NEG_BIG)
                return before + within[tq - 1:tq, :]

            lax.fori_loop(0, width // tq, block, jnp.zeros((1, tq), _F32))

        def logits(h):
            s = lax.dot_general(kb_ref[h, :width, :], qb_ref[h], _NT, preferred_element_type=_F32)
            return s + bias_ref[:width, :]

        def finish(h, s):
            mx = jnp.max(s, axis=0, keepdims=True)
            p = jnp.exp2(s - mx)
            den = jnp.sum(p, axis=0, keepdims=True)
            ot = jnp.dot(vt_ref[h, :, :width], p.astype(_BF16), preferred_element_type=_F32)
            o_ref[h] = (ot / den).T.astype(_BF16)

        group_h = _pick(hd, (4, 2, 1))

        def heads(g, c):
            h0 = g * group_h
            s_next = logits(h0)
            for u in range(group_h):
                s_cur = s_next
                if u + 1 < group_h:
                    s_next = logits(h0 + u + 1)
                finish(h0 + u, s_cur)
            return c

        lax.fori_loop(0, hd // group_h, heads, 0)

    nq = seq // tq
    per = nq // nbuckets
    bucket = i // per
    for bi in range(nbuckets):
        pl.when(bucket == bi)(functools.partial(run, (bi + 1) * per * tq))


def _dsa_attention(proj, wi, lay, batch, seq, hi, hd, tq):
    m = batch * seq
    nq = seq // tq
    topk = min(TOPK_MAX, seq // 4)
    nbuckets = 4 if nq % 4 == 0 else 1
    npairs = hi // 2
    qi_b = lay["qi"][0] // npairs
    ke_b = lay["misc"][0] // 2
    qb_b, kb_b, vb_b = (lay[n][0] // hd for n in ("qb", "kb", "vb"))
    assert lay["qi"][0] % npairs == 0 and lay["misc"][0] % 2 == 0
    assert all(lay[n][0] % hd == 0 for n in ("qb", "kb", "vb"))
    kern = functools.partial(_dsa_kernel, tq=tq, seq=seq, hi=hi, hd=hd, topk=topk,
                             nbuckets=nbuckets)
    r = lax.broadcasted_iota(jnp.int32, (tq, tq), 0)
    c = lax.broadcasted_iota(jnp.int32, (tq, tq), 1)
    lowtri = jnp.where(c <= r, 1.0, 0.0).astype(_BF16)
    return pl.pallas_call(
        kern,
        out_shape=jax.ShapeDtypeStruct((hd, m, HEAD_DIM), _BF16),
        grid=(batch, nq),
        in_specs=[
            pl.BlockSpec((npairs, tq, LANES), lambda b, i: (qi_b, b * nq + i, 0)),
            pl.BlockSpec((2, seq, LANES), lambda b, i: (ke_b, b, 0)),
            pl.BlockSpec((tq, LANES), lambda b, i: (b * nq + i, 0)),
            pl.BlockSpec((hd, tq, HEAD_DIM), lambda b, i: (qb_b, b * nq + i, 0)),
            pl.BlockSpec((hd, seq, HEAD_DIM), lambda b, i: (kb_b, b, 0)),
            pl.BlockSpec((hd, seq, HEAD_DIM), lambda b, i: (vb_b, b, 0)),
            pl.BlockSpec((tq, tq), lambda b, i: (0, 0)),
        ],
        out_specs=pl.BlockSpec((hd, tq, HEAD_DIM), lambda b, i: (0, b * nq + i, 0)),
        scratch_shapes=[pltpu.VMEM((LANES, tq), _F32),
                        pltpu.VMEM((hd, HEAD_DIM, seq), _BF16),
                        pltpu.VMEM((seq, tq), _F32),
                        pltpu.VMEM((seq, tq), jnp.int32),
                        pltpu.VMEM((seq, tq), jnp.int16),
                        pltpu.VMEM((seq, tq), _F32)],
        compiler_params=_cparams(2),
        name="dsa_attn",
    )(proj, proj, wi, proj, proj, proj, lowtri)


def _mix_out_kernel(ya_ref, yb_ref, wa_ref, wb_ref, ga_ref, gb_ref, wo_ref, x_ref, gt_ref,
                    o_ref, mg_ref, *, hd, tm):
    rc = min(ROW_CHUNK, tm)
    for r in range(tm // rc):
        rows = slice(r * rc, (r + 1) * rc)
        yb = jnp.concatenate([yb_ref[h, rows, :] for h in range(hd)], axis=1)
        a = jnp.dot(ya_ref[rows, :], wa_ref[...], preferred_element_type=_F32)
        b = jnp.dot(yb, wb_ref[...], preferred_element_type=_F32)
        for c in range(a.shape[1] // LANES):
            sl = slice(c * LANES, (c + 1) * LANES)
            mg_ref[rows, sl] = (ga_ref[c, rows, :].astype(_F32) * a[:, sl]
                                + gb_ref[c, rows, :].astype(_F32) * b[:, sl]).astype(_BF16)
    for r in range(tm // rc):
        rows = slice(r * rc, (r + 1) * rc)
        acc = jnp.dot(mg_ref[rows, :], wo_ref[...], preferred_element_type=_F32)
        o_ref[rows, :] = x_ref[rows, :] + (1.0 + gt_ref[...]) * acc


def _mix_out(ya, yb, wa, wb, wo, layer, proj, x2, gt, lay, seq, tm):
    m, ka = ya.shape
    hd = yb.shape[0]
    d = wo.shape[2]
    nchunks = d // LANES
    tps = seq // tm
    assert lay["ga"][0] % nchunks == 0 and lay["gb"][0] % nchunks == 0
    gate = lambda name: pl.BlockSpec((nchunks, tm, LANES),
                                     lambda i: (lay[name][0] // nchunks, i, 0))
    resident = lambda rows: pl.BlockSpec((None, rows, d), lambda i: (layer, 0, 0),
                                         pipeline_mode=pl.Buffered(1))
    return pl.pallas_call(
        functools.partial(_mix_out_kernel, hd=hd, tm=tm),
        out_shape=jax.ShapeDtypeStruct((m, d), _F32),
        grid=(m // tm,),
        in_specs=[
            pl.BlockSpec((tm, ka), lambda i: (i, 0)),
            pl.BlockSpec((hd, tm, HEAD_DIM), lambda i: (0, i, 0)),
            resident(ka), resident(hd * HEAD_DIM),
            gate("ga"), gate("gb"),
            resident(d),
            pl.BlockSpec((tm, d), lambda i: (i, 0)),
            pl.BlockSpec((None, 1, d), lambda i: (i // tps, 0, 0)),
        ],
        out_specs=pl.BlockSpec((tm, d), lambda i: (i, 0)),
        scratch_shapes=[pltpu.VMEM((tm, d), _BF16)],
        compiler_params=_cparams(1),
        name="mix_out",
    )(ya, yb, wa, wb, proj, proj, wo, x2, gt)


def _out_res_kernel(a_ref, w_ref, x_ref, gt_ref, o_ref):
    acc = jnp.dot(a_ref[...], w_ref[...], preferred_element_type=_F32)
    o_ref[...] = x_ref[...] + (1.0 + gt_ref[...]) * acc


def _out_res(a, w, layer, x2, gt, seq, tm, tn):
    m, k = a.shape
    d = w.shape[2]
    tps = seq // tm
    return pl.pallas_call(
        _out_res_kernel,
        out_shape=jax.ShapeDtypeStruct((m, d), _F32),
        grid=(m // tm, d // tn),
        in_specs=[
            pl.BlockSpec((tm, k), lambda i, j: (i, 0)),
            pl.BlockSpec((None, k, tn), lambda i, j: (layer, 0, j),
                         pipeline_mode=pl.Buffered(1 if tn == d else 2)),
            pl.BlockSpec((tm, tn), lambda i, j: (i, j)),
            pl.BlockSpec((None, 1, tn), lambda i, j: (i // tps, 0, j)),
        ],
        out_specs=pl.BlockSpec((tm, tn), lambda i, j: (i, j)),
        compiler_params=_cparams(2),
        name="out_res",
    )(a, w, x2, gt)


def _shift_rows(u, k, prev):
    rolled = pltpu.roll(u, k, axis=0)
    first = lax.broadcasted_iota(jnp.int32, prev.shape, 0) < k
    head = jnp.where(first, pltpu.roll(prev, k, axis=0), rolled[:SUBLANES])
    return jnp.concatenate([head, rolled[SUBLANES:]], axis=0)


def _ffn_up_kernel(x_ref, g_ref, sc_ref, sh_ref, *refs, tm, tn, tps, cps, nblk):
    w_refs, (cp_ref, o_ref, h_ref, halo_ref, carry_ref) = refs[:2 * cps], refs[2 * cps:]
    i = pl.program_id(0)
    j = pl.program_id(1)
    rc = min(FFN_ROW_CHUNK, tm)

    @pl.when(j == 0)
    def _():
        h = _norm_modulate(x_ref[...], g_ref[...], sc_ref[...], sh_ref[...])
        h_ref[...] = h.astype(_BF16)

    def column_block(sub):
        jj = j * cps + sub
        wa_ref, wb_ref = w_refs[2 * sub], w_refs[2 * sub + 1]

        @pl.when(i % tps == 0)
        def _():
            halo_ref[...] = jnp.zeros_like(halo_ref)

        @pl.when(i % tps != 0)
        def _():
            halo_ref[...] = carry_ref[jj]

        cp = cp_ref[jj]

        def conv(u, prev, cols):
            u1 = _shift_rows(u, 1, prev)
            u2 = _shift_rows(u, 2, prev)
            return cp[3:4, cols] + (cp[0:1, cols] * u2 + cp[1:2, cols] * u1 + cp[2:3, cols] * u)

        prev_a = halo_ref[:, 0:tn]
        prev_b = halo_ref[:, tn:2 * tn]
        for r in range(tm // rc):
            lo = r * rc
            hr = h_ref[lo:lo + rc, :]
            ua = jnp.dot(hr, wa_ref[...], preferred_element_type=_F32)
            ub = jnp.dot(hr, wb_ref[...], preferred_element_type=_F32)
            a = conv(ua, prev_a, slice(0, tn))
            b = conv(ub, prev_b, slice(tn, 2 * tn))
            o_ref[lo:lo + rc, sub * tn:(sub + 1) * tn] = (a * jax.nn.sigmoid(a) * b).astype(_BF16)
            prev_a = ua[rc - SUBLANES:, :]
            prev_b = ub[rc - SUBLANES:, :]

        carry_ref[jj] = jnp.concatenate([prev_a, prev_b], axis=1)

    for sub in range(cps):
        if sub == 0 or nblk % cps == 0:
            column_block(sub)
        else:
            pl.when(j * cps + sub < nblk)(functools.partial(column_block, sub))


def _conv_params(conv_w, conv_b, tn):
    depth, taps, width = conv_w.shape
    nblk = width // (2 * tn)
    rows = jnp.concatenate([conv_w, conv_b[:, None, :],
                            jnp.zeros((depth, SUBLANES - taps - 1, width), conv_w.dtype)], axis=1)
    rows = rows.reshape(depth, SUBLANES, 2, nblk, tn)
    return jnp.transpose(rows, (0, 3, 1, 2, 4)).reshape(depth, nblk, SUBLANES, 2 * tn)


def _ffn_up(x2, g, sc, sh, w, cp, layer, seq, tm, tn):
    m, d = x2.shape
    nblk = w.shape[2] // (2 * tn)
    cps = FFN_BLOCKS_PER_STEP
    tps = seq // tm
    mod_spec = pl.BlockSpec((None, 1, d), lambda i, j: (i // tps, 0, 0))

    def halves(sub):
        col = lambda j: jnp.minimum(j * cps + sub, nblk - 1)
        return (pl.BlockSpec((None, d, tn), lambda i, j: (layer, 0, col(j))),
                pl.BlockSpec((None, d, tn), lambda i, j: (layer, 0, nblk + col(j))))

    w_specs = [s for sub in range(cps) for s in halves(sub)]
    return pl.pallas_call(
        functools.partial(_ffn_up_kernel, tm=tm, tn=tn, tps=tps, cps=cps, nblk=nblk),
        out_shape=jax.ShapeDtypeStruct((m, nblk * tn), _BF16),
        grid=(m // tm, pl.cdiv(nblk, cps)),
        in_specs=[
            pl.BlockSpec((tm, d), lambda i, j: (i, 0)),
            pl.BlockSpec((None, 1, d), lambda i, j: (layer, 0, 0)),
            mod_spec, mod_spec,
            *w_specs,
            pl.BlockSpec((None, nblk, SUBLANES, 2 * tn), lambda i, j: (layer, 0, 0, 0)),
        ],
        out_specs=pl.BlockSpec((tm, cps * tn), lambda i, j: (i, j)),
        scratch_shapes=[pltpu.VMEM((tm, d), _BF16),
                        pltpu.VMEM((SUBLANES, 2 * tn), _F32),
                        pltpu.VMEM((nblk, SUBLANES, 2 * tn), _F32)],
        compiler_params=_cparams(2),
        name="ffn_up",
    )(x2, g, sc, sh, *([w] * (2 * cps)), cp)


def _final_norm_kernel(x_ref, g_ref, o_ref):
    x = x_ref[...]
    ms = jnp.mean(x * x, axis=-1, keepdims=True)
    o_ref[...] = (x * lax.rsqrt(ms + NORM_EPS)) * g_ref[...]


def _final_norm(x2, g, tm):
    m, d = x2.shape
    return pl.pallas_call(
        _final_norm_kernel,
        out_shape=jax.ShapeDtypeStruct((m, d), _F32),
        grid=(m // tm,),
        in_specs=[pl.BlockSpec((tm, d), lambda i: (i, 0)),
                  pl.BlockSpec((1, d), lambda i: (0, 0))],
        out_specs=pl.BlockSpec((tm, d), lambda i: (i, 0)),
        compiler_params=_cparams(1),
        name="final_norm",
    )(x2, g)


def _arrange_w_in(w_in, d, hs, hd, hi, misc_chunks):
    sw, dw, iw = hs * HEAD_DIM, hd * HEAD_DIM, hi * IDX_DIM
    sizes = (sw, sw, sw, dw, dw, dw, iw, IDX_DIM, hi, d, d)
    offs = [0]
    for s in sizes:
        offs.append(offs[-1] + s)
    qa, ka, va, qb, kb, vb, qi, ki, wi, ga, gb = (
        w_in[..., offs[n]:offs[n + 1]] for n in range(len(sizes)))
    z = lambda n: jnp.zeros(w_in.shape[:-1] + (n,), w_in.dtype)
    misc = jnp.concatenate([ki, z(LANES - IDX_DIM), z(LANES - IDX_DIM), ki,
                            wi, z(LANES - hi), z((misc_chunks - 3) * LANES)], axis=-1)
    return jnp.concatenate([ga, gb, qa, ka, va, qb, kb, vb, qi, misc], axis=-1).astype(_BF16)


def _rope_tables(positions):
    pos = positions.reshape(-1).astype(_F32)

    def cos_sin(dim):
        inv_freq = 1.0 / (ROPE_THETA ** (jnp.arange(0, dim, 2, dtype=_F32) / dim))
        ang = pos[:, None] * inv_freq
        return jnp.cos(ang), jnp.sin(ang)

    c, s = cos_sin(HEAD_DIM)
    c128 = jnp.concatenate([c, c], axis=-1)
    s128 = jnp.concatenate([-s, s], axis=-1)
    c, s = cos_sin(IDX_DIM)
    zero = jnp.zeros_like(s)
    c64 = jnp.concatenate([c, c, c, c], axis=-1)
    s64lo = jnp.concatenate([-s, zero, -s, zero], axis=-1)
    s64hi = jnp.concatenate([zero, s, zero, s], axis=-1)
    return c128, s128, c64, s64lo, s64hi


def _pick(n, prefs):
    for p in prefs:
        if n % p == 0:
            return p
    return n


def kernel(x, c, positions, w_in, w_a, w_b, w_o, w_ada, b_ada, g_mix, g_ffn, w_up, conv_w,
           conv_b, w_down, g_final):
    batch, seq, d = x.shape
    depth = w_in.shape[0]
    hs = w_a.shape[1] // HEAD_DIM
    hd = w_b.shape[1] // HEAD_DIM
    rest = w_in.shape[2] - 3 * hs * HEAD_DIM - 3 * hd * HEAD_DIM - 2 * d - IDX_DIM
    hi = rest // (IDX_DIM + 1)
    dff = w_down.shape[1]
    m = batch * seq
    lay = _proj_layout(d, hs, hd, hi)

    tm = _pick(seq, (1024, 512, 256, 128))
    tm_mix = _pick(seq, (512, 256, 128))
    t_sb = _pick(seq, (256, 128))
    nh_sb = 2
    tq_dsa = _pick(seq, (256, 128))
    tn_ffn = _pick(dff, (512, 256, 128))
    tn_out = _pick(d, (512, 256, 128))

    w_in_r = _arrange_w_in(w_in, d, hs, hd, hi, lay["misc"][1])
    w_a16, w_b16, w_o16, w_up16, w_down16 = (t.astype(_BF16)
                                             for t in (w_a, w_b, w_o, w_up, w_down))
    tabs = _rope_tables(positions)
    mod = _modulation(c, w_ada, b_ada).reshape(depth, batch, N_MOD, 1, d)
    g_mix3, g_ffn3 = g_mix.reshape(depth, 1, d), g_ffn.reshape(depth, 1, d)
    conv_p = _conv_params(conv_w, conv_b, tn_ffn)

    x2 = x.reshape(m, d)
    for l in range(depth):
        sh1, sc1, gt1, sh2, sc2, gt2 = (mod[l, :, n] for n in range(N_MOD))
        proj, wi = _in_proj(x2, g_mix3, sc1, sh1, w_in_r, l, tabs, lay, hi, seq, tm)
        ya = _sb_attention(proj, lay, batch, seq, hs, t_sb, nh_sb)
        yb = _dsa_attention(proj, wi, lay, batch, seq, hi, hd, tq_dsa)
        x2 = _mix_out(ya, yb, w_a16, w_b16, w_o16, l, proj, x2, gt1, lay, seq, tm_mix)
        g = _ffn_up(x2, g_ffn3, sc2, sh2, w_up16, conv_p, l, seq, tm, tn_ffn)
        x2 = _out_res(g, w_down16, l, x2, gt2, seq, tm_mix, d)
    out = _final_norm(x2, g_final.reshape(1, d), tm)
    return out.reshape(batch, seq, d)
```
